```python
import math
import jax, jax.numpy as jnp
from jax import lax
import numpy as np

D_MODEL = 2048
BATCH = 2
SEQ = 4096
DEPTH = 4
DEC_BATCH = 128
DEC_SEQ = 1
PAST_LEN = 8192
PAGE_SIZE = 128

N_MIXERS = 3
N_LAYERS_A = (DEPTH + 2) // 3
N_LAYERS_B = (DEPTH + 1) // 3
N_LAYERS_C = DEPTH // 3

A_HEADS = 16
A_NOPE = 128
A_ROPE = 64
A_VDIM = 128
A_Q_LORA = 512
A_KV_LORA = 512
ROPE_THETA = 10000.0
Q_BLOCK = 128

B_HEADS = 16
B_DK = 128
B_DV = D_MODEL // B_HEADS
B_CHUNK = 64

C_WIDTH = 2048
C_GROUPS = 16
C_GDIM = C_WIDTH // C_GROUPS
C_CHUNK = 128

D_FF = 5632
EPS = 1e-6

kernel_name = "hybrid_mla_hgrn2_gmlp_macaron_step"


def rms_norm(x, g):
    xf = x.astype(jnp.float32)
    y = xf * lax.rsqrt(jnp.mean(xf * xf, axis=-1, keepdims=True) + EPS)
    return (y * g.astype(jnp.float32)).astype(x.dtype)


def swiglu(x, wi, wo):
    gate, up = jnp.split(x @ wi, 2, axis=-1)
    return (jax.nn.silu(gate) * up) @ wo


def rope(x, pos):
    half = A_ROPE // 2
    inv = ROPE_THETA ** (-jnp.arange(half, dtype=jnp.float32) / half)
    ang = pos.astype(jnp.float32)[:, None] * inv[None, :]
    ang = ang.reshape(ang.shape[:1] + (1,) * (x.ndim - 3) + ang.shape[1:])
    cos, sin = jnp.cos(ang), jnp.sin(ang)
    xf = x.astype(jnp.float32)
    x1, x2 = xf[..., :half], xf[..., half:]
    return jnp.concatenate([x1 * cos - x2 * sin, x2 * cos + x1 * sin], axis=-1).astype(x.dtype)


def mla_project(h, pos, w_in, g_q, g_kv, w_uq):
    B, T, _ = h.shape
    cq, ckv, kr = jnp.split(h @ w_in, [A_Q_LORA, A_Q_LORA + A_KV_LORA], axis=-1)
    cq = rms_norm(cq, g_q)
    ckv = rms_norm(ckv, g_kv)
    q = (cq @ w_uq).reshape(B, T, A_HEADS, A_NOPE + A_ROPE)
    q_nope = q[..., :A_NOPE]
    q_rope = rope(q[..., A_NOPE:], pos)
    kr = rope(kr, pos)
    return q_nope, q_rope, ckv, kr


def mla_prompt(h, w_in, g_q, g_kv, w_uq, w_ukv, w_o):
    B, T, _ = h.shape
    pos = jnp.arange(T)
    q_nope, q_rope, ckv, kr = mla_project(h, pos, w_in, g_q, g_kv, w_uq)
    kv = (ckv @ w_ukv).reshape(B, T, A_HEADS, A_NOPE + A_VDIM)
    k_nope, v = kv[..., :A_NOPE], kv[..., A_NOPE:]
    scale = (A_NOPE + A_ROPE) ** -0.5
    nb = T // Q_BLOCK
    qn_b = q_nope.reshape(B, nb, Q_BLOCK, A_HEADS, A_NOPE).transpose(1, 0, 2, 3, 4)
    qr_b = q_rope.reshape(B, nb, Q_BLOCK, A_HEADS, A_ROPE).transpose(1, 0, 2, 3, 4)
    kpos = jnp.arange(T)

    def block(args):
        qn, qr, bi = args
        s = (jnp.einsum('bqhd,bkhd->bhqk', qn, k_nope).astype(jnp.float32)
             + jnp.einsum('bqhr,bkr->bhqk', qr, kr).astype(jnp.float32)) * scale
        qpos = bi * Q_BLOCK + jnp.arange(Q_BLOCK)
        s = jnp.where(kpos[None, :] <= qpos[:, None], s, -jnp.inf)
        p = jax.nn.softmax(s, axis=-1).astype(v.dtype)
        return jnp.einsum('bhqk,bkhd->bqhd', p, v)

    o = lax.map(block, (qn_b, qr_b, jnp.arange(nb)))
    o = o.transpose(1, 0, 2, 3, 4).reshape(B, T, A_HEADS * A_VDIM)
    return o @ w_o, ckv, kr


def mla_sample(h, ckv_pool, kr_pool, page_table, w_in, g_q, g_kv, w_uq, w_ukv, w_o):
    Bd, T, _ = h.shape
    pos = PAST_LEN + jnp.arange(T)
    q_nope, q_rope, ckv, kr = mla_project(h, pos, w_in, g_q, g_kv, w_uq)
    w_ukv_h = w_ukv.reshape(A_KV_LORA, A_HEADS, A_NOPE + A_VDIM)
    w_uk, w_uv = w_ukv_h[..., :A_NOPE], w_ukv_h[..., A_NOPE:]
    q_lat = jnp.einsum('bthd,chd->bthc', q_nope, w_uk)
    scale = (A_NOPE + A_ROPE) ** -0.5
    causal = jnp.arange(T)[None, :] <= jnp.arange(T)[:, None]

    def attend_one(args):
        ql, qr, c_new, r_new, pages = args
        past_c = ckv_pool[pages].reshape(-1, A_KV_LORA)
        past_r = kr_pool[pages].reshape(-1, A_ROPE)
        s_past = (jnp.einsum('thc,kc->htk', ql, past_c).astype(jnp.float32)
                  + jnp.einsum('thr,kr->htk', qr, past_r).astype(jnp.float32))
        s_new = (jnp.einsum('thc,kc->htk', ql, c_new).astype(jnp.float32)
                 + jnp.einsum('thr,kr->htk', qr, r_new).astype(jnp.float32))
        s_new = jnp.where(causal[None], s_new, -jnp.inf)
        s = jnp.concatenate([s_past, s_new], axis=-1) * scale
        p = jax.nn.softmax(s, axis=-1).astype(ql.dtype)
        P = past_c.shape[0]
        return (jnp.einsum('htk,kc->thc', p[..., :P], past_c)
                + jnp.einsum('htk,kc->thc', p[..., P:], c_new))

    o_lat = lax.map(attend_one, (q_lat, q_rope, ckv, kr, page_table))
    o = jnp.einsum('bthc,chv->bthv', o_lat, w_uv).reshape(Bd, T, A_HEADS * A_VDIM)
    return o @ w_o, ckv, kr


def gla_chunk_scan(q, k, v, logf, s0):
    B, T, H, DK = q.shape
    DV = v.shape[-1]
    L = min(B_CHUNK, T)
    n = -(-T // L)
    pad = n * L - T
    f32 = jnp.float32

    def prep(a):
        a = a.astype(f32)
        if pad:
            a = jnp.pad(a, ((0, 0), (0, pad), (0, 0), (0, 0)))
        return a.reshape(B, n, L, H, a.shape[-1]).transpose(1, 0, 2, 3, 4)

    tri = jnp.arange(L)[:, None] >= jnp.arange(L)[None, :]

    def step(S, xs):
        qc, kc, vc, gc = xs
        b = jnp.cumsum(gc, axis=1)
        o_inter = jnp.einsum('blhk,bhkv->blhv', qc * jnp.exp(b), S)
        diff = b[:, :, None] - b[:, None, :]
        decay = jnp.exp(jnp.where(tri[None, :, :, None, None], diff, -jnp.inf))
        A = jnp.einsum('bthk,bshk,btshk->bhts', qc, kc, decay)
        o_intra = jnp.einsum('bhts,bshv->bthv', A, vc)
        bL = b[:, -1]
        kd = kc * jnp.exp(bL[:, None] - b)
        S_new = jnp.exp(bL)[..., None] * S + jnp.einsum('bshk,bshv->bhkv', kd, vc)
        return S_new, o_inter + o_intra

    S, o = lax.scan(step, s0.astype(f32), (prep(q), prep(k), prep(v), prep(logf)))
    o = o.transpose(1, 0, 2, 3, 4).reshape(B, n * L, H, DV)[:, :T]
    return o, S


def hgrn2_mixer(h, s0, lb, w_in, g_o, w_o):
    B, T, _ = h.shape
    HK, HV = B_HEADS * B_DK, B_HEADS * B_DV
    q, f, i, og = jnp.split(h @ w_in, [HK, 2 * HK, 2 * HK + HV], axis=-1)
    gate = lb + (1.0 - lb) * jax.nn.sigmoid(f.astype(jnp.float32))
    logf = jnp.log(gate)
    k = 1.0 - gate
    shp = (B, T, B_HEADS, B_DK)
    o, S = gla_chunk_scan(jax.nn.silu(q).reshape(shp), k.reshape(shp),
                          i.reshape(B, T, B_HEADS, B_DV), logf.reshape(shp), s0)
    o = rms_norm(o, g_o.reshape(B_HEADS, B_DV)).reshape(B, T, HV).astype(h.dtype)
    o = o * jax.nn.silu(og)
    return o @ w_o, S.astype(s0.dtype)


def chunk_gmlp(h, w_in, g_v, w_s, b_s, w_o):
    B, T, _ = h.shape
    u, v = jnp.split(jax.nn.gelu(h @ w_in), 2, axis=-1)
    v = rms_norm(v, g_v)
    Lc = C_CHUNK if T > C_CHUNK else T
    n = -(-T // Lc)
    pad = n * Lc - T
    vp = jnp.pad(v, ((0, 0), (0, pad), (0, 0))) if pad else v
    vp = vp.reshape(B, n, Lc, C_GROUPS, C_GDIM)
    tril = jnp.tril(jnp.ones((Lc, Lc), dtype=bool))
    ws = jnp.where(tril[None], w_s[:, :Lc, :Lc], 0.0)
    mixed = (jnp.einsum('gts,bnsgd->bntgd', ws, vp)
             + b_s[:, :Lc].T[:, :, None]).reshape(B, n * Lc, C_WIDTH)[:, :T]
    y = (u * mixed) @ w_o
    last = ((T - 1) // C_CHUNK) * C_CHUNK
    return y, v[:, last:]


def setup_inputs(seed: int = 0) -> dict:
    key = jax.random.key(seed)
    keys = jax.random.split(key, 40)
    cnt = iter(range(40))
    f32 = jnp.float32

    def nk():
        return keys[next(cnt)]

    def w(shape, fan_in):
        return jax.random.normal(nk(), shape, f32) * (fan_in ** -0.5)

    def gain(shape):
        return 1.0 + 0.05 * jax.random.normal(nk(), shape, f32)

    n_pages = PAST_LEN // PAGE_SIZE
    n_used = DEC_BATCH * n_pages
    n_pool = n_used + n_used // 4
    perm = jax.random.permutation(nk(), n_pool)
    page_table = perm[:n_used].reshape(DEC_BATCH, n_pages).astype(jnp.int32)

    HK, HV = B_HEADS * B_DK, B_HEADS * B_DV
    return {
        "x_prompt": jax.random.normal(nk(), (BATCH, SEQ, D_MODEL), f32),
        "x_sample": jax.random.normal(nk(), (DEC_BATCH, DEC_SEQ, D_MODEL), f32),
        "cache_a_ckv": jax.random.normal(nk(), (N_LAYERS_A, n_pool, PAGE_SIZE, A_KV_LORA), f32),
        "cache_a_kr": jax.random.normal(nk(), (N_LAYERS_A, n_pool, PAGE_SIZE, A_ROPE), f32),
        "state_b": 0.5 * jax.random.normal(nk(), (N_LAYERS_B, DEC_BATCH, B_HEADS, B_DK, B_DV), f32),
        "page_table": page_table,
        "norm_ffn_a": gain((DEPTH, D_MODEL)),
        "ffn_a_wi": w((DEPTH, D_MODEL, 2 * D_FF), D_MODEL),
        "ffn_a_wo": w((DEPTH, D_FF, D_MODEL), D_FF),
        "norm_mix": gain((DEPTH, D_MODEL)),
        "a_w_in": w((N_LAYERS_A, D_MODEL, A_Q_LORA + A_KV_LORA + A_ROPE), D_MODEL),
        "a_g_q": gain((N_LAYERS_A, A_Q_LORA)),
        "a_g_kv": gain((N_LAYERS_A, A_KV_LORA)),
        "a_w_uq": w((N_LAYERS_A, A_Q_LORA, A_HEADS * (A_NOPE + A_ROPE)), A_Q_LORA),
        "a_w_ukv": w((N_LAYERS_A, A_KV_LORA, A_HEADS * (A_NOPE + A_VDIM)), A_KV_LORA),
        "a_w_o": w((N_LAYERS_A, A_HEADS * A_VDIM, D_MODEL), A_HEADS * A_VDIM),
        "b_w_in": w((N_LAYERS_B, D_MODEL, 2 * HK + 2 * HV), D_MODEL),
        "b_lower_bounds": 0.5 * jax.random.normal(nk(), (DEPTH, HK), f32),
        "b_g_o": gain((N_LAYERS_B, HV)),
        "b_w_o": w((N_LAYERS_B, HV, D_MODEL), HV),
        "c_w_in": w((N_LAYERS_C, D_MODEL, 2 * C_WIDTH), D_MODEL),
        "c_g_v": gain((N_LAYERS_C, C_WIDTH)),
        "c_w_s": w((N_LAYERS_C, C_GROUPS, C_CHUNK, C_CHUNK), C_CHUNK),
        "c_b_s": gain((N_LAYERS_C, C_GROUPS, C_CHUNK)),
        "c_w_o": w((N_LAYERS_C, C_WIDTH, D_MODEL), C_WIDTH),
        "norm_ffn_b": gain((DEPTH, D_MODEL)),
        "ffn_b_wi": w((DEPTH, D_MODEL, 2 * D_FF), D_MODEL),
        "ffn_b_wo": w((DEPTH, D_FF, D_MODEL), D_FF),
        "final_norm": gain((D_MODEL,)),
    }


def reference(x_prompt, x_sample, cache_a_ckv, cache_a_kr, state_b, page_table,
              norm_ffn_a, ffn_a_wi, ffn_a_wo, norm_mix,
              a_w_in, a_g_q, a_g_kv, a_w_uq, a_w_ukv, a_w_o,
              b_w_in, b_lower_bounds, b_g_o, b_w_o,
              c_w_in, c_g_v, c_w_s, c_b_s, c_w_o,
              norm_ffn_b, ffn_b_wi, ffn_b_wo, final_norm):
    lb_sm = jax.nn.softmax(b_lower_bounds.astype(jnp.float32), axis=0)
    lb_all = jnp.cumsum(lb_sm, axis=0) - lb_sm[0]

    xp, xs = x_prompt, x_sample
    a_ckv_p, a_kr_p, a_ckv_s, a_kr_s = [], [], [], []
    b_sp, b_ss, c_vp, c_vs = [], [], [], []
    for i in range(DEPTH):
        xp = xp + 0.5 * swiglu(rms_norm(xp, norm_ffn_a[i]), ffn_a_wi[i], ffn_a_wo[i])
        xs = xs + 0.5 * swiglu(rms_norm(xs, norm_ffn_a[i]), ffn_a_wi[i], ffn_a_wo[i])
        hp = rms_norm(xp, norm_mix[i])
        hs = rms_norm(xs, norm_mix[i])
        j = i // N_MIXERS
        kind = i % N_MIXERS
        if kind == 0:
            yp, ckv_p, kr_p = mla_prompt(hp, a_w_in[j], a_g_q[j], a_g_kv[j], a_w_uq[j], a_w_ukv[j], a_w_o[j])
            ys, ckv_s, kr_s = mla_sample(hs, cache_a_ckv[j], cache_a_kr[j], page_table, a_w_in[j],
                                         a_g_q[j], a_g_kv[j], a_w_uq[j], a_w_ukv[j], a_w_o[j])
            a_ckv_p.append(ckv_p); a_kr_p.append(kr_p)
            a_ckv_s.append(ckv_s); a_kr_s.append(kr_s)
        elif kind == 1:
            s0 = jnp.zeros((xp.shape[0], B_HEADS, B_DK, B_DV), jnp.float32)
            yp, sp = hgrn2_mixer(hp, s0, lb_all[i], b_w_in[j], b_g_o[j], b_w_o[j])
            ys, ss = hgrn2_mixer(hs, state_b[j], lb_all[i], b_w_in[j], b_g_o[j], b_w_o[j])
            b_sp.append(sp); b_ss.append(ss)
        else:
            yp, vp = chunk_gmlp(hp, c_w_in[j], c_g_v[j], c_w_s[j], c_b_s[j], c_w_o[j])
            ys, vs = chunk_gmlp(hs, c_w_in[j], c_g_v[j], c_w_s[j], c_b_s[j], c_w_o[j])
            c_vp.append(vp); c_vs.append(vs)
        xp = xp + yp
        xs = xs + ys
        xp = xp + 0.5 * swiglu(rms_norm(xp, norm_ffn_b[i]), ffn_b_wi[i], ffn_b_wo[i])
        xs = xs + 0.5 * swiglu(rms_norm(xs, norm_ffn_b[i]), ffn_b_wi[i], ffn_b_wo[i])

    y_prompt = rms_norm(xp, final_norm)
    y_sample = rms_norm(xs, final_norm)
    return (y_prompt, y_sample,
            jnp.stack(a_ckv_p), jnp.stack(a_kr_p), jnp.stack(a_ckv_s), jnp.stack(a_kr_s),
            jnp.stack(b_sp), jnp.stack(b_ss), jnp.stack(c_vp), jnp.stack(c_vs))
```

```python
import functools
import math

import jax
import jax.numpy as jnp
from jax import lax
from jax.experimental import pallas as pl
from jax.experimental.pallas import tpu as pltpu

F32 = jnp.float32
BF16 = jnp.bfloat16

EPS = 1e-6
ROPE_THETA = 10000.0
N_MIXERS = 3

A_HEADS = 16
A_NOPE = 128
A_ROPE = 64
A_VDIM = 128
A_Q_LORA = 512
A_KV_LORA = 512
A_QK_PAD = 256
PAGE_SIZE = 128

B_HEADS = 16
B_DK = 128
B_DV = 128
B_SUB = 16

C_GROUPS = 16
C_GDIM = 128
C_CHUNK = 128

MIB = 1024 * 1024


def _cparams(semantics, vmem_mib=48):
    return pltpu.CompilerParams(dimension_semantics=semantics,
                                vmem_limit_bytes=vmem_mib * MIB)


def _rms(x, g):
    return x * lax.rsqrt(jnp.mean(x * x, axis=-1, keepdims=True) + EPS) * g


def _sigmoid(x):
    return 1.0 / (1.0 + jnp.exp(-x))


def _silu(x):
    return x * _sigmoid(x)


def _gelu_tanh(x):
    c = math.sqrt(2.0 / math.pi)
    return 0.5 * x * (1.0 + jnp.tanh(c * (x + 0.044715 * (x * x * x))))


def _row_tile(m, cap):
    best = None
    for t in range(16, cap + 1, 16):
        if m % t == 0:
            best = t
    assert best is not None, (m, cap)
    return best


def _norm_mm_kernel(x_ref, g_ref, *refs, n_w, epilogue):
    w_refs = refs[:n_w]
    o_ref = refs[n_w]
    h_ref = refs[n_w + 1]

    @pl.when(pl.program_id(1) == 0)
    def _():
        h_ref[...] = _rms(x_ref[...], g_ref[...]).astype(BF16)

    h = h_ref[...]
    accs = [jnp.dot(h, w[...], preferred_element_type=F32) for w in w_refs]
    o_ref[...] = epilogue(*accs).astype(o_ref.dtype)


def norm_mm(x, g, w, col_offsets, out_cols, tn, epilogue, out_dtype, name, tm_cap=1040):
    M, D = x.shape
    tm = _row_tile(M, tm_cap)
    n_w = len(col_offsets)
    in_specs = [pl.BlockSpec((tm, D), lambda i, j: (i, 0)),
                pl.BlockSpec((1, D), lambda i, j: (0, 0))]
    for off in col_offsets:
        assert off % tn == 0
        in_specs.append(pl.BlockSpec((D, tn), lambda i, j, o=off // tn: (0, j + o)))
    return pl.pallas_call(
        functools.partial(_norm_mm_kernel, n_w=n_w, epilogue=epilogue),
        grid=(M // tm, out_cols // tn),
        in_specs=in_specs,
        out_specs=pl.BlockSpec((tm, tn), lambda i, j: (i, j)),
        out_shape=jax.ShapeDtypeStruct((M, out_cols), out_dtype),
        scratch_shapes=[pltpu.VMEM((tm, D), BF16)],
        compiler_params=_cparams(("parallel", "arbitrary")),
        name=name,
    )(x, g.reshape(1, D), *([w] * n_w))


def _mm_kernel(a_ref, w_ref, *refs, scale, has_res):
    o_ref = refs[-1]
    acc = jnp.dot(a_ref[...], w_ref[...], preferred_element_type=F32)
    if has_res:
        acc = refs[0][...] + scale * acc
    o_ref[...] = acc.astype(o_ref.dtype)


def mm(a, w, out_dtype, name, res=None, scale=1.0, tm_cap=1040, tn=512):
    M, K = a.shape
    N = w.shape[1]
    tm = _row_tile(M, tm_cap)
    tn = min(tn, N)
    in_specs = [pl.BlockSpec((tm, K), lambda i, j: (i, 0)),
                pl.BlockSpec((K, tn), lambda i, j: (0, j))]
    args = [a, w]
    if res is not None:
        in_specs.append(pl.BlockSpec((tm, tn), lambda i, j: (i, j)))
        args.append(res)
    return pl.pallas_call(
        functools.partial(_mm_kernel, scale=scale, has_res=res is not None),
        grid=(M // tm, N // tn),
        in_specs=in_specs,
        out_specs=pl.BlockSpec((tm, tn), lambda i, j: (i, j)),
        out_shape=jax.ShapeDtypeStruct((M, N), out_dtype),
        compiler_params=_cparams(("parallel", "parallel")),
        name=name,
    )(*args)


def _head_mm_kernel(x_ref, w_ref, o_ref, *, heads, k, n):
    for h in range(heads):
        o_ref[:, h * n:(h + 1) * n] = jnp.dot(
            x_ref[:, h * k:(h + 1) * k], w_ref[h], preferred_element_type=F32).astype(o_ref.dtype)


def head_mm(x, w, out_dtype, name):
    R = x.shape[0]
    H, K, N = w.shape
    return pl.pallas_call(
        functools.partial(_head_mm_kernel, heads=H, k=K, n=N),
        grid=(1,),
        in_specs=[pl.BlockSpec((R, H * K), lambda i: (0, 0)),
                  pl.BlockSpec((H, K, N), lambda i: (0, 0, 0))],
        out_specs=pl.BlockSpec((R, H * N), lambda i: (0, 0)),
        out_shape=jax.ShapeDtypeStruct((R, H * N), out_dtype),
        compiler_params=_cparams(("arbitrary",)),
        name=name,
    )(x, w)


def _norm_kernel(x_ref, g_ref, o_ref):
    o_ref[...] = _rms(x_ref[...], g_ref[...])


def final_norm_call(x, g):
    M, D = x.shape
    tm = _row_tile(M, 1040)
    return pl.pallas_call(
        _norm_kernel,
        grid=(M // tm,),
        in_specs=[pl.BlockSpec((tm, D), lambda i: (i, 0)),
                  pl.BlockSpec((1, D), lambda i: (0, 0))],
        out_specs=pl.BlockSpec((tm, D), lambda i: (i, 0)),
        out_shape=jax.ShapeDtypeStruct((M, D), F32),
        compiler_params=_cparams(("parallel",)),
        name="final_norm",
    )(x, g.reshape(1, D))


def _swiglu_epilogue(gate, up):
    return _silu(gate) * up


def ffn_half_step(x, g, wi, wo):
    d_ff = wo.shape[0]
    hid = norm_mm(x, g, wi, (0, d_ff), d_ff, 512, _swiglu_epilogue, BF16, "ffn_in")
    return mm(hid, wo, F32, "ffn_out", res=x, scale=0.5, tm_cap=832, tn=512)


def _mla_in_kernel(x_ref, g_ref, w_ref, gq_ref, gkv_ref, cos_ref, sin_ref,
                   cq_ref, ckv_ref, ckvb_ref, kr_ref, krb_ref):
    h = _rms(x_ref[...], g_ref[...]).astype(BF16)
    acc = jnp.dot(h, w_ref[...], preferred_element_type=F32)
    ql, kl = A_Q_LORA, A_KV_LORA
    cq_ref[...] = _rms(acc[:, :ql], gq_ref[...]).astype(BF16)
    ckv = _rms(acc[:, ql:ql + kl], gkv_ref[...])
    ckv_ref[...] = ckv
    ckvb_ref[...] = ckv.astype(BF16)
    kr = (acc[:, ql + kl:ql + kl + 128] * cos_ref[...]
          + acc[:, ql + kl + 128:ql + kl + 256] * sin_ref[...])
    kr_ref[...] = kr
    krb_ref[...] = kr.astype(BF16)


def mla_in(x, g, w_ext, g_q, g_kv, cos_t, sin_t):
    M, D = x.shape
    tm = _row_tile(M, 640)
    NW = w_ext.shape[1]
    row = lambda n: pl.BlockSpec((tm, n), lambda i: (i, 0))
    full = lambda a, b: pl.BlockSpec((a, b), lambda i: (0, 0))
    return pl.pallas_call(
        _mla_in_kernel,
        grid=(M // tm,),
        in_specs=[row(D), full(1, D), full(D, NW), full(1, A_Q_LORA), full(1, A_KV_LORA),
                  row(128), row(128)],
        out_specs=[row(A_Q_LORA), row(A_KV_LORA), row(A_KV_LORA), row(128), row(128)],
        out_shape=[jax.ShapeDtypeStruct((M, A_Q_LORA), BF16),
                   jax.ShapeDtypeStruct((M, A_KV_LORA), F32),
                   jax.ShapeDtypeStruct((M, A_KV_LORA), BF16),
                   jax.ShapeDtypeStruct((M, 128), F32),
                   jax.ShapeDtypeStruct((M, 128), BF16)],
        compiler_params=_cparams(("parallel",)),
        name="mla_in",
    )(x, g.reshape(1, D), w_ext, g_q.reshape(1, -1), g_kv.reshape(1, -1), cos_t, sin_t)


def _mla_q_kernel(cq_ref, w1_ref, w2_ref, cos_ref, sin_ref, o_ref):
    cq = cq_ref[...]
    cos = cos_ref[...]
    sin = sin_ref[...]
    P = A_QK_PAD
    for h in range(A_HEADS):
        a1 = jnp.dot(cq, w1_ref[:, h * P:(h + 1) * P], preferred_element_type=F32)
        a2 = jnp.dot(cq, w2_ref[:, h * 128:(h + 1) * 128], preferred_element_type=F32)
        o_ref[:, h * P:h * P + 128] = a1[:, :128].astype(BF16)
        o_ref[:, h * P + 128:(h + 1) * P] = (a1[:, 128:] * cos + a2 * sin).astype(BF16)


def mla_q(cq, w1, w2, cos_t, sin_t):
    M = cq.shape[0]
    tm = _row_tile(M, 640)
    row = lambda n: pl.BlockSpec((tm, n), lambda i: (i, 0))
    full = lambda a, b: pl.BlockSpec((a, b), lambda i: (0, 0))
    return pl.pallas_call(
        _mla_q_kernel,
        grid=(M // tm,),
        in_specs=[row(A_Q_LORA), full(*w1.shape), full(*w2.shape), row(128), row(128)],
        out_specs=row(A_HEADS * A_QK_PAD),
        out_shape=jax.ShapeDtypeStruct((M, A_HEADS * A_QK_PAD), BF16),
        compiler_params=_cparams(("parallel",)),
        name="mla_q",
    )(cq, w1, w2, cos_t, sin_t)


def _flash_kernel(q_ref, kn_ref, kr_ref, v_ref, o_ref, m_ref, l_ref, acc_ref, *, scale):
    qi = pl.program_id(2)
    ki = pl.program_id(3)

    @pl.when(ki == 0)
    def _():
        m_ref[...] = jnp.full(m_ref.shape, -jnp.inf, F32)
        l_ref[...] = jnp.zeros(l_ref.shape, F32)
        acc_ref[...] = jnp.zeros(acc_ref.shape, F32)

    def step(masked):
        q = q_ref[...]
        k = jnp.concatenate([kn_ref[...], kr_ref[...]], axis=1)
        s = lax.dot_general(q, k, (((1,), (1,)), ((), ())), preferred_element_type=F32) * scale
        if masked:
            row = lax.broadcasted_iota(jnp.int32, s.shape, 0)
            col = lax.broadcasted_iota(jnp.int32, s.shape, 1)
            s = jnp.where(col <= row, s, -jnp.inf)
        m_prev = m_ref[...]
        m_new = jnp.maximum(m_prev, jnp.max(s, axis=-1, keepdims=True))
        alpha = jnp.exp(m_prev - m_new)
        p = jnp.exp(s - m_new)
        l_ref[...] = alpha * l_ref[...] + jnp.sum(p, axis=-1, keepdims=True)
        acc_ref[...] = alpha * acc_ref[...] + jnp.dot(p.astype(BF16), v_ref[...],
                                                       preferred_element_type=F32)
        m_ref[...] = m_new

    @pl.when(ki < qi)
    def _():
        step(False)

    @pl.when(ki == qi)
    def _():
        step(True)
        o_ref[...] = (acc_ref[...] / l_ref[...]).astype(o_ref.dtype)


def mla_flash(qp, kv, krb, n_batch, seq, tile):
    nt = seq // tile
    H = A_HEADS
    scale = (A_NOPE + A_ROPE) ** -0.5
    kmap = lambda off: (lambda b, h, qi, ki: (b * nt + jnp.minimum(ki, qi), h + off))
    return pl.pallas_call(
        functools.partial(_flash_kernel, scale=scale),
        grid=(n_batch, H, nt, nt),
        in_specs=[pl.BlockSpec((tile, A_QK_PAD), lambda b, h, qi, ki: (b * nt + qi, h)),
                  pl.BlockSpec((tile, A_NOPE), kmap(0)),
                  pl.BlockSpec((tile, 128), lambda b, h, qi, ki: (b * nt + jnp.minimum(ki, qi), 0)),
                  pl.BlockSpec((tile, A_VDIM), kmap(H))],
        out_specs=pl.BlockSpec((tile, A_VDIM), lambda b, h, qi, ki: (b * nt + qi, h)),
        out_shape=jax.ShapeDtypeStruct((n_batch * seq, H * A_VDIM), BF16),
        scratch_shapes=[pltpu.VMEM((tile, 1), F32), pltpu.VMEM((tile, 1), F32),
                        pltpu.VMEM((tile, A_VDIM), F32)],
        compiler_params=_cparams(("parallel", "parallel", "parallel", "arbitrary")),
        name="mla_flash",
    )(qp, kv, krb, kv)


def _decode_kernel(pt_ref, ql_ref, qr_ref, cn_ref, rn_ref, ckv_hbm, kr_hbm, o_ref,
                   cbuf, rbuf, sems, m_ref, l_ref, acc_ref,
                   *, layer, n_pages, chunk, scale):
    g = pl.program_id(0)
    n_steps = pl.num_programs(0)
    n_chunks = n_pages // chunk
    c = g % n_chunks
    slot = g % 2

    def copies(step, slot_):
        out = []
        base = step * chunk
        for p in range(chunk):
            page = pt_ref[base + p]
            out.append(pltpu.make_async_copy(ckv_hbm.at[layer, page], cbuf.at[slot_, p],
                                             sems.at[0, slot_, p]))
            out.append(pltpu.make_async_copy(kr_hbm.at[layer, page], rbuf.at[slot_, p],
                                             sems.at[1, slot_, p]))
        return out

    @pl.when(g == 0)
    def _():
        for cp in copies(0, 0):
            cp.start()

    @pl.when(g + 1 < n_steps)
    def _():
        for cp in copies(g + 1, 1 - slot):
            cp.start()

    ql = ql_ref[0]
    qr = qr_ref[0]

    @pl.when(c == 0)
    def _():
        cn = cn_ref[0].astype(F32)
        rn = rn_ref[0].astype(F32)
        s_new = (jnp.sum(ql.astype(F32) * cn, axis=-1, keepdims=True)
                 + jnp.sum(qr.astype(F32) * rn, axis=-1, keepdims=True)) * scale
        m_ref[...] = s_new
        l_ref[...] = jnp.ones(l_ref.shape, F32)
        acc_ref[...] = jnp.broadcast_to(cn, acc_ref.shape)

    for cp in copies(g, slot):
        cp.wait()

    cb = cbuf[slot].reshape(chunk * PAGE_SIZE, A_KV_LORA).astype(BF16)
    rb = rbuf[slot].reshape(chunk * PAGE_SIZE, A_ROPE).astype(BF16)
    dn = (((1,), (1,)), ((), ()))
    s = (lax.dot_general(ql, cb, dn, preferred_element_type=F32)
         + lax.dot_general(qr, rb, dn, preferred_element_type=F32)) * scale
    m_prev = m_ref[...]
    m_new = jnp.maximum(m_prev, jnp.max(s, axis=-1, keepdims=True))
    alpha = jnp.exp(m_prev - m_new)
    p = jnp.exp(s - m_new)
    l_ref[...] = alpha * l_ref[...] + jnp.sum(p, axis=-1, keepdims=True)
    acc_ref[...] = alpha * acc_ref[...] + jnp.dot(p.astype(BF16), cb, preferred_element_type=F32)
    m_ref[...] = m_new

    @pl.when(c == n_chunks - 1)
    def _():
        o_ref[0] = (acc_ref[...] / l_ref[...]).astype(o_ref.dtype)


def mla_decode(page_table, ql, qr, cn, rn, cache_ckv, cache_kr, layer, chunk=8):
    Bd, n_pages = page_table.shape
    assert n_pages % chunk == 0
    n_chunks = n_pages // chunk
    H, C, R = A_HEADS, A_KV_LORA, A_ROPE
    scale = (A_NOPE + A_ROPE) ** -0.5
    req = lambda shp: pl.BlockSpec((1,) + shp, lambda g, pt: (g // n_chunks, 0, 0))
    grid_spec = pltpu.PrefetchScalarGridSpec(
        num_scalar_prefetch=1,
        grid=(Bd * n_chunks,),
        in_specs=[req((H, C)), req((H, R)), req((1, C)), req((1, R)),
                  pl.BlockSpec(memory_space=pl.ANY), pl.BlockSpec(memory_space=pl.ANY)],
        out_specs=req((H, C)),
        scratch_shapes=[pltpu.VMEM((2, chunk, PAGE_SIZE, C), F32),
                        pltpu.VMEM((2, chunk, PAGE_SIZE, R), F32),
                        pltpu.SemaphoreType.DMA((2, 2, chunk)),
                        pltpu.VMEM((H, 1), F32), pltpu.VMEM((H, 1), F32),
                        pltpu.VMEM((H, C), F32)])
    return pl.pallas_call(
        functools.partial(_decode_kernel, layer=layer, n_pages=n_pages, chunk=chunk, scale=scale),
        grid_spec=grid_spec,
        out_shape=jax.ShapeDtypeStruct((Bd, H, C), BF16),
        compiler_params=_cparams(("arbitrary",)),
        name="mla_decode",
    )(page_table.reshape(-1), ql, qr, cn, rn, cache_ckv, cache_kr)


def _rope_tables(pos):
    half = A_ROPE // 2
    inv = ROPE_THETA ** (-jnp.arange(half, dtype=F32) / half)
    ang = pos.astype(F32)[:, None] * inv[None, :]
    cos, sin = jnp.cos(ang), jnp.sin(ang)
    z = jnp.zeros((pos.shape[0], 128 - A_ROPE), F32)
    return (jnp.concatenate([cos, cos, z], axis=1), jnp.concatenate([sin, sin, z], axis=1))


def _half_swap(w):
    half = A_ROPE // 2
    return jnp.concatenate([-w[..., half:], w[..., :half]], axis=-1)


def mla_layer(x, g_mix, w_in, g_q, g_kv, w_uq, w_ukv, w_o, cache_ckv, cache_kr, page_table,
              layer, n_batch, seq, cos_t, sin_t, flash_tile):
    D = x.shape[1]
    Mp = n_batch * seq
    H = A_HEADS
    lo = A_Q_LORA + A_KV_LORA
    zpad = jnp.zeros((D, 128 - A_ROPE), F32)
    w_kr = w_in[:, lo:]
    w_in_ext = jnp.concatenate([w_in[:, :lo], w_kr, zpad, _half_swap(w_kr), zpad], axis=1).astype(BF16)
    cq, ckv, ckvb, kr, krb = mla_in(x, g_mix, w_in_ext, g_q, g_kv, cos_t, sin_t)

    wq = w_uq.reshape(A_Q_LORA, H, A_NOPE + A_ROPE)
    wq_rope = wq[..., A_NOPE:]
    zq = jnp.zeros((A_Q_LORA, H, 128 - A_ROPE), F32)
    w1 = jnp.concatenate([wq[..., :A_NOPE], wq_rope, zq], axis=-1).reshape(A_Q_LORA, H * A_QK_PAD)
    w2 = jnp.concatenate([_half_swap(wq_rope), zq], axis=-1).reshape(A_Q_LORA, H * 128)
    qp = mla_q(cq, w1.astype(BF16), w2.astype(BF16), cos_t, sin_t)

    wkv = w_ukv.reshape(A_KV_LORA, H, A_NOPE + A_VDIM)
    w_uk = wkv[..., :A_NOPE]
    w_uv = wkv[..., A_NOPE:]
    w_kv_cols = jnp.concatenate([w_uk.reshape(A_KV_LORA, H * A_NOPE),
                                 w_uv.reshape(A_KV_LORA, H * A_VDIM)], axis=1).astype(BF16)
    kv = mm(ckvb, w_kv_cols, BF16, "mla_kv", tn=1024)
    o_p = mla_flash(qp, kv, krb, n_batch, seq, flash_tile)

    Bd = page_table.shape[0]
    qs = qp[Mp:].reshape(Bd, H, A_QK_PAD)
    q_nope = qs[:, :, :A_NOPE].reshape(Bd, H * A_NOPE)
    q_rope = qs[:, :, A_NOPE:A_NOPE + A_ROPE]
    q_lat = head_mm(q_nope, jnp.transpose(w_uk, (1, 2, 0)).astype(BF16), BF16, "mla_qlat")
    o_lat = mla_decode(page_table, q_lat.reshape(Bd, H, A_KV_LORA), q_rope,
                       ckvb[Mp:].reshape(Bd, 1, A_KV_LORA),
                       krb[Mp:, :A_ROPE].reshape(Bd, 1, A_ROPE),
                       cache_ckv, cache_kr, layer)
    o_s = head_mm(o_lat.reshape(Bd, H * A_KV_LORA), jnp.transpose(w_uv, (1, 0, 2)).astype(BF16), BF16,
                  "mla_ouv")

    o = jnp.concatenate([o_p, o_s], axis=0)
    x = mm(o, w_o.astype(BF16), F32, "mla_out", res=x, scale=1.0, tn=1024)
    return x, ckv, kr[:, :A_ROPE]


def _lower_bound(raw, layer):
    e = jnp.exp(raw - jnp.max(raw, axis=0, keepdims=True))
    tot = jnp.sum(e, axis=0, keepdims=True)
    if layer == 0:
        return jnp.zeros_like(tot)
    return jnp.sum(e[1:layer + 1], axis=0, keepdims=True) / tot


def _hgrn_scan_kernel(q_ref, f_ref, i_ref, og_ref, lbr_ref, go_ref, o_ref, st_ref,
                      cum_ref, state_ref, qs_ref, k_ref, *, layer, tb):
    t = pl.program_id(2)

    @pl.when(t == 0)
    def _():
        state_ref[...] = jnp.zeros(state_ref.shape, F32)

    lb = _lower_bound(lbr_ref[...], layer)
    gate = lb + (1.0 - lb) * _sigmoid(f_ref[...])
    logf = jnp.log(gate)
    r_id = lax.broadcasted_iota(jnp.int32, (tb, tb), 0)
    c_id = lax.broadcasted_iota(jnp.int32, (tb, tb), 1)
    tri = jnp.where(r_id >= c_id, 1.0, 0.0).astype(BF16)
    a1 = logf.astype(BF16)
    r1 = logf - a1.astype(F32)
    a2 = r1.astype(BF16)
    a3 = (r1 - a2.astype(F32)).astype(BF16)
    cum = (jnp.dot(tri, a1, preferred_element_type=F32)
           + jnp.dot(tri, a2, preferred_element_type=F32)
           + jnp.dot(tri, a3, preferred_element_type=F32))
    cum_ref[0:8, :] = jnp.zeros((8, B_DK), F32)
    cum_ref[8:8 + tb, :] = cum
    qs_ref[...] = _silu(q_ref[...])
    k_ref[...] = 1.0 - gate

    go = go_ref[...]
    rows = lax.broadcasted_iota(jnp.int32, (B_SUB, 1), 0)

    def sub_chunk(c, carry):
        r0 = pl.multiple_of(c * B_SUB, B_SUB)
        b = cum_ref[pl.ds(r0 + 8, B_SUB), :] - cum_ref[pl.ds(r0 + 7, 1), :]
        q = qs_ref[pl.ds(r0, B_SUB), :]
        k = k_ref[pl.ds(r0, B_SUB), :]
        v = i_ref[pl.ds(r0, B_SUB), :]
        st = state_ref[...]
        b_last = b[B_SUB - 1:B_SUB, :]
        o = lax.dot_general((q * jnp.exp(b)).astype(BF16), st.astype(BF16),
                            (((1,), (1,)), ((), ())), preferred_element_type=F32)
        kd = (k * jnp.exp(b_last - b)).astype(BF16)
        inc = lax.dot_general(v.astype(BF16), kd, (((0,), (0,)), ((), ())),
                              preferred_element_type=F32)
        state_ref[...] = st * jnp.exp(b_last) + inc
        for s in range(B_SUB):
            d = jnp.where(rows >= s, b - b[s:s + 1, :], -1e30)
            a_col = jnp.sum(q * jnp.exp(d) * k[s:s + 1, :], axis=-1, keepdims=True)
            o = o + a_col * v[s:s + 1, :]
        o = _rms(o, go) * _silu(og_ref[pl.ds(r0, B_SUB), :])
        o_ref[pl.ds(r0, B_SUB), :] = o.astype(o_ref.dtype)
        return carry

    lax.fori_loop(0, tb // B_SUB, sub_chunk, 0)

    @pl.when(t == pl.num_programs(2) - 1)
    def _():
        st_ref[0, 0] = state_ref[...].T


def hgrn_scan(proj, lb_raw, g_o, layer, n_batch, seq, tb):
    H = B_HEADS
    nt = seq // tb
    sec = lambda k: pl.BlockSpec((tb, 128), lambda b, h, t, k=k: (b * nt + t, k * H + h))
    return pl.pallas_call(
        functools.partial(_hgrn_scan_kernel, layer=layer, tb=tb),
        grid=(n_batch, H, nt),
        in_specs=[sec(0), sec(1), sec(2), sec(3),
                  pl.BlockSpec((lb_raw.shape[0], 128), lambda b, h, t: (0, h)),
                  pl.BlockSpec((1, 128), lambda b, h, t: (0, h))],
        out_specs=[pl.BlockSpec((tb, 128), lambda b, h, t: (b * nt + t, h)),
                   pl.BlockSpec((1, 1, B_DK, B_DV), lambda b, h, t: (b, h, 0, 0))],
        out_shape=[jax.ShapeDtypeStruct((n_batch * seq, H * B_DV), BF16),
                   jax.ShapeDtypeStruct((n_batch, H, B_DK, B_DV), F32)],
        scratch_shapes=[pltpu.VMEM((8 + tb, B_DK), F32), pltpu.VMEM((B_DV, B_DK), F32),
                        pltpu.VMEM((tb, B_DK), F32), pltpu.VMEM((tb, B_DK), F32)],
        compiler_params=_cparams(("parallel", "parallel", "arbitrary")),
        name="hgrn_scan",
    )(proj, proj, proj, proj, lb_raw, g_o.reshape(1, -1))


def _hgrn_step_kernel(p_ref, s_ref, lbr_ref, go_ref, o_ref, so_ref, *, layer):
    H = B_HEADS
    p = p_ref[0]
    lb = _lower_bound(lbr_ref[...], layer)[0]
    gate = lb + (1.0 - lb) * _sigmoid(p[H:2 * H])
    q = _silu(p[0:H])
    v = p[2 * H:3 * H]
    og = p[3 * H:4 * H]
    packed = jnp.concatenate([gate, q, jnp.zeros((128 - 2 * H, B_DK), F32)], axis=0)
    cols = packed.T
    outs = []
    for h in range(H):
        g_col = cols[:, h:h + 1]
        q_col = cols[:, H + h:H + h + 1]
        s_new = g_col * s_ref[0, h] + (1.0 - g_col) * v[h:h + 1, :]
        so_ref[0, h] = s_new
        outs.append(jnp.sum(q_col * s_new, axis=0, keepdims=True))
    o = jnp.concatenate(outs, axis=0)
    o_ref[0] = (_rms(o, go_ref[...]) * _silu(og)).astype(o_ref.dtype)


def hgrn_step(proj_s, state, lb_raw, g_o, layer):
    Bd = proj_s.shape[0]
    H = B_HEADS
    depth = lb_raw.shape[0]
    o, s_new = pl.pallas_call(
        functools.partial(_hgrn_step_kernel, layer=layer),
        grid=(Bd,),
        in_specs=[pl.BlockSpec((1, 4 * H, 128), lambda r: (r, 0, 0)),
                  pl.BlockSpec((1, H, B_DK, B_DV), lambda r: (r, 0, 0, 0)),
                  pl.BlockSpec((depth, H, B_DK), lambda r: (0, 0, 0)),
                  pl.BlockSpec((H, B_DV), lambda r: (0, 0))],
        out_specs=[pl.BlockSpec((1, H, B_DV), lambda r: (r, 0, 0)),
                   pl.BlockSpec((1, H, B_DK, B_DV), lambda r: (r, 0, 0, 0))],
        out_shape=[jax.ShapeDtypeStruct((Bd, H, B_DV), BF16),
                   jax.ShapeDtypeStruct(state.shape, F32)],
        compiler_params=_cparams(("parallel",)),
        name="hgrn_step",
    )(proj_s.reshape(Bd, 4 * H, 128), state, lb_raw.reshape(depth, H, B_DK), g_o.reshape(H, B_DV))
    return o.reshape(Bd, H * B_DV), s_new


def hgrn_layer(x, g_mix, w_in, lb_raw, g_o, w_o, state, layer, n_batch, seq, scan_tb):
    Mp = n_batch * seq
    proj = norm_mm(x, g_mix, w_in.astype(BF16), (0,), w_in.shape[1], 1024, lambda a: a, F32, "hgrn_in")
    o_p, st_p = hgrn_scan(proj, lb_raw, g_o, layer, n_batch, seq, scan_tb)
    o_s, st_s = hgrn_step(proj[Mp:], state, lb_raw, g_o, layer)
    o = jnp.concatenate([o_p, o_s], axis=0)
    x = mm(o, w_o.astype(BF16), F32, "hgrn_out", res=x, scale=1.0, tn=1024)
    return x, st_p, st_s


def _gmlp_mix_kernel(u_ref, v_ref, gv_ref, ws_ref, bias_ref, vo_ref, z_ref):
    v = _rms(v_ref[...], gv_ref[...])
    vo_ref[...] = v
    vb = v.astype(BF16)
    L = v.shape[0]
    r_id = lax.broadcasted_iota(jnp.int32, (L, L), 0)
    c_id = lax.broadcasted_iota(jnp.int32, (L, L), 1)
    for g in range(C_GROUPS):
        sl = slice(g * C_GDIM, (g + 1) * C_GDIM)
        w = jnp.where(r_id >= c_id, ws_ref[g], 0.0).astype(BF16)
        mixed = jnp.dot(w, vb[:, sl], preferred_element_type=F32) + bias_ref[:, sl]
        z_ref[:, sl] = (u_ref[:, sl] * mixed).astype(z_ref.dtype)


def gmlp_mix(uv, g_v, w_s, bias_full, n_rows):
    W = g_v.shape[0]
    L = C_CHUNK
    return pl.pallas_call(
        _gmlp_mix_kernel,
        grid=(n_rows // L,),
        in_specs=[pl.BlockSpec((L, W), lambda c: (c, 0)),
                  pl.BlockSpec((L, W), lambda c: (c, 1)),
                  pl.BlockSpec((1, W), lambda c: (0, 0)),
                  pl.BlockSpec((C_GROUPS, L, L), lambda c: (0, 0, 0)),
                  pl.BlockSpec((L, W), lambda c: (0, 0))],
        out_specs=[pl.BlockSpec((L, W), lambda c: (c, 0)),
                   pl.BlockSpec((L, W), lambda c: (c, 0))],
        out_shape=[jax.ShapeDtypeStruct((n_rows, W), F32),
                   jax.ShapeDtypeStruct((n_rows, W), BF16)],
        compiler_params=_cparams(("parallel",)),
        name="gmlp_mix",
    )(uv, uv, g_v.reshape(1, W), w_s, bias_full)


def _gmlp_single_kernel(u_ref, v_ref, gv_ref, w0_ref, b0_ref, vo_ref, z_ref):
    v = _rms(v_ref[...], gv_ref[...])
    vo_ref[...] = v
    z_ref[...] = (u_ref[...] * (w0_ref[...] * v + b0_ref[...])).astype(z_ref.dtype)


def gmlp_single(uv_s, g_v, w0, b0):
    R = uv_s.shape[0]
    W = g_v.shape[0]
    blk = lambda j: pl.BlockSpec((R, W), lambda i, j=j: (0, j))
    vec = pl.BlockSpec((1, W), lambda i: (0, 0))
    return pl.pallas_call(
        _gmlp_single_kernel,
        grid=(1,),
        in_specs=[blk(0), blk(1), vec, vec, vec],
        out_specs=[blk(0), blk(0)],
        out_shape=[jax.ShapeDtypeStruct((R, W), F32), jax.ShapeDtypeStruct((R, W), BF16)],
        compiler_params=_cparams(("arbitrary",)),
        name="gmlp_single",
    )(uv_s, uv_s, g_v.reshape(1, W), w0, b0)


def gmlp_layer(x, g_mix, w_in, g_v, w_s, b_s, w_o, n_batch, seq):
    Mp = n_batch * seq
    W = g_v.shape[0]
    uv = norm_mm(x, g_mix, w_in.astype(BF16), (0,), w_in.shape[1], 1024, _gelu_tanh, F32, "gmlp_in")
    assert seq % C_CHUNK == 0
    bias_full = jnp.repeat(b_s[:, :C_CHUNK].T, C_GDIM, axis=1)
    v_p, z_p = gmlp_mix(uv, g_v, w_s[:, :C_CHUNK, :C_CHUNK], bias_full, Mp)
    w0 = jnp.repeat(w_s[:, 0, 0], C_GDIM).reshape(1, W)
    b0 = jnp.repeat(b_s[:, 0], C_GDIM).reshape(1, W)
    v_s, z_s = gmlp_single(uv[Mp:], g_v, w0, b0)
    z = jnp.concatenate([z_p, z_s], axis=0)
    x = mm(z, w_o.astype(BF16), F32, "gmlp_out", res=x, scale=1.0, tn=1024)
    last = ((seq - 1) // C_CHUNK) * C_CHUNK
    v_last = v_p.reshape(n_batch, seq, W)[:, last:]
    return x, v_last, v_s


def _forward(x_prompt, x_sample, cache_a_ckv, cache_a_kr, state_b, page_table, norm_ffn_a,
             ffn_a_wi, ffn_a_wo, norm_mix, a_w_in, a_g_q, a_g_kv, a_w_uq, a_w_ukv, a_w_o,
             b_w_in, b_lower_bounds, b_g_o, b_w_o, c_w_in, c_g_v, c_w_s, c_b_s, c_w_o,
             norm_ffn_b, ffn_b_wi, ffn_b_wo, final_norm, *, flash_tile, scan_tb):
    n_batch, seq, D = x_prompt.shape
    Bd, dec_seq, _ = x_sample.shape
    assert dec_seq == 1
    depth = norm_mix.shape[0]
    Mp = n_batch * seq
    past_len = page_table.shape[1] * PAGE_SIZE
    x = jnp.concatenate([x_prompt.reshape(Mp, D), x_sample.reshape(Bd, D)], axis=0)

    pos = jnp.concatenate([jnp.tile(jnp.arange(seq), n_batch),
                           jnp.full((Bd,), past_len, jnp.int32)])
    cos_t, sin_t = _rope_tables(pos)

    a_ckv, a_kr, b_sp, b_ss, c_vp, c_vs = [], [], [], [], [], []
    for i in range(depth):
        x = ffn_half_step(x, norm_ffn_a[i], ffn_a_wi[i].astype(BF16), ffn_a_wo[i].astype(BF16))
        j = i // N_MIXERS
        kind = i % N_MIXERS
        if kind == 0:
            x, ckv, kr = mla_layer(x, norm_mix[i], a_w_in[j], a_g_q[j], a_g_kv[j], a_w_uq[j],
                                   a_w_ukv[j], a_w_o[j], cache_a_ckv, cache_a_kr, page_table,
                                   j, n_batch, seq, cos_t, sin_t, flash_tile)
            a_ckv.append(ckv)
            a_kr.append(kr)
        elif kind == 1:
            x, sp, ss = hgrn_layer(x, norm_mix[i], b_w_in[j], b_lower_bounds, b_g_o[j], b_w_o[j],
                                   state_b[j], i, n_batch, seq, scan_tb)
            b_sp.append(sp)
            b_ss.append(ss)
        else:
            x, vp, vs = gmlp_layer(x, norm_mix[i], c_w_in[j], c_g_v[j], c_w_s[j], c_b_s[j],
                                   c_w_o[j], n_batch, seq)
            c_vp.append(vp)
            c_vs.append(vs)
        x = ffn_half_step(x, norm_ffn_b[i], ffn_b_wi[i].astype(BF16), ffn_b_wo[i].astype(BF16))

    y = final_norm_call(x, final_norm)
    ckv_all = jnp.stack(a_ckv)
    kr_all = jnp.stack(a_kr)
    return (y[:Mp].reshape(n_batch, seq, D), y[Mp:].reshape(Bd, 1, D),
            ckv_all[:, :Mp].reshape(-1, n_batch, seq, A_KV_LORA),
            kr_all[:, :Mp].reshape(-1, n_batch, seq, A_ROPE),
            ckv_all[:, Mp:].reshape(-1, Bd, 1, A_KV_LORA),
            kr_all[:, Mp:].reshape(-1, Bd, 1, A_ROPE),
            jnp.stack(b_sp), jnp.stack(b_ss),
            jnp.stack(c_vp), jnp.stack(c_vs)[:, :, None, :])


def kernel(x_prompt, x_sample, cache_a_ckv, cache_a_kr, state_b, page_table, norm_ffn_a, ffn_a_wi, ffn_a_wo, norm_mix, a_w_in, a_g_q, a_g_kv, a_w_uq, a_w_ukv, a_w_o, b_w_in, b_lower_bounds, b_g_o, b_w_o, c_w_in, c_g_v, c_w_s, c_b_s, c_w_o, norm_ffn_b, ffn_b_wi, ffn_b_wo, final_norm):
    seq = x_prompt.shape[1]
    return _forward(x_prompt, x_sample, cache_a_ckv, cache_a_kr, state_b, page_table, norm_ffn_a,
                    ffn_a_wi, ffn_a_wo, norm_mix, a_w_in, a_g_q, a_g_kv, a_w_uq, a_w_ukv, a_w_o,
                    b_w_in, b_lower_bounds, b_g_o, b_w_o, c_w_in, c_g_v, c_w_s, c_b_s, c_w_o,
                    norm_ffn_b, ffn_b_wi, ffn_b_wo, final_norm,
                    flash_tile=min(1024, seq), scan_tb=min(256, seq))
```

```python
import functools
import math

import jax
import jax.numpy as jnp
from jax import lax
from jax.experimental import pallas as pl
from jax.experimental.pallas import tpu as pltpu

F32 = jnp.float32
BF16 = jnp.bfloat16

EPS = 1e-6
ROPE_THETA = 10000.0
N_MIXERS = 3

A_HEADS = 16
A_NOPE = 128
A_ROPE = 64
A_VDIM = 128
A_Q_LORA = 512
A_KV_LORA = 512
A_QK_PAD = 256
PAGE_SIZE = 128

B_HEADS = 16
B_DK = 128
B_DV = 128
B_SUB = 16

C_GROUPS = 16
C_GDIM = 128
C_CHUNK = 128

MIB = 1024 * 1024


def _cparams(semantics, vmem_mib=48):
    return pltpu.CompilerParams(dimension_semantics=semantics,
                                vmem_limit_bytes=vmem_mib * MIB)


def _rms(x, g):
    return x * lax.rsqrt(jnp.mean(x * x, axis=-1, keepdims=True) + EPS) * g


def _sigmoid(x):
    return 1.0 / (1.0 + jnp.exp(-x))


def _silu(x):
    return x * _sigmoid(x)


def _gelu_tanh(x):
    c = math.sqrt(2.0 / math.pi)
    return 0.5 * x * (1.0 + jnp.tanh(c * (x + 0.044715 * (x * x * x))))


def _row_tile(m, cap):
    best = None
    for t in range(16, cap + 1, 16):
        if m % t == 0:
            best = t
    assert best is not None, (m, cap)
    return best


def _norm_mm_kernel(x_ref, g_ref, *refs, n_w, epilogue):
    w_refs = refs[:n_w]
    o_ref = refs[n_w]
    h_ref = refs[n_w + 1]

    @pl.when(pl.program_id(1) == 0)
    def _():
        h_ref[...] = _rms(x_ref[...], g_ref[...]).astype(BF16)

    h = h_ref[...]
    accs = [jnp.dot(h, w[...].astype(BF16), preferred_element_type=F32) for w in w_refs]
    o_ref[...] = epilogue(*accs).astype(o_ref.dtype)


def _weight_spec(w, layer, rows, tn, col_block):
    if layer is None:
        assert w.ndim == 2
        return pl.BlockSpec((rows, tn), lambda i, j: (0, col_block(i, j)))
    assert w.ndim == 3
    return pl.BlockSpec((None, rows, tn), lambda i, j: (layer, 0, col_block(i, j)))


def norm_mm(x, g, w, col_offsets, out_cols, tn, epilogue, out_dtype, name, layer=None, tm_cap=1040):
    M, D = x.shape
    tm = _row_tile(M, tm_cap)
    n_w = len(col_offsets)
    in_specs = [pl.BlockSpec((tm, D), lambda i, j: (i, 0)),
                pl.BlockSpec((1, D), lambda i, j: (0, 0))]
    for off in col_offsets:
        assert off % tn == 0
        in_specs.append(_weight_spec(w, layer, D, tn, lambda i, j, o=off // tn: j + o))
    return pl.pallas_call(
        functools.partial(_norm_mm_kernel, n_w=n_w, epilogue=epilogue),
        grid=(M // tm, out_cols // tn),
        in_specs=in_specs,
        out_specs=pl.BlockSpec((tm, tn), lambda i, j: (i, j)),
        out_shape=jax.ShapeDtypeStruct((M, out_cols), out_dtype),
        scratch_shapes=[pltpu.VMEM((tm, D), BF16)],
        compiler_params=_cparams(("parallel", "arbitrary")),
        name=name,
    )(x, g.reshape(1, D), *([w] * n_w))


def _mm_kernel(a_ref, w_ref, *refs, scale, has_res):
    o_ref = refs[-1]
    acc = jnp.dot(a_ref[...], w_ref[...].astype(BF16), preferred_element_type=F32)
    if has_res:
        acc = refs[0][...] + scale * acc
    o_ref[...] = acc.astype(o_ref.dtype)


def mm(a, w, out_dtype, name, res=None, scale=1.0, layer=None, tm_cap=1040, tn=512, vmem_mib=48):
    M, K = a.shape
    N = w.shape[-1]
    tm = _row_tile(M, tm_cap)
    tn = min(tn, N)
    in_specs = [pl.BlockSpec((tm, K), lambda i, j: (i, 0)),
                _weight_spec(w, layer, K, tn, lambda i, j: j)]
    args = [a, w]
    if res is not None:
        in_specs.append(pl.BlockSpec((tm, tn), lambda i, j: (i, j)))
        args.append(res)
    return pl.pallas_call(
        functools.partial(_mm_kernel, scale=scale, has_res=res is not None),
        grid=(M // tm, N // tn),
        in_specs=in_specs,
        out_specs=pl.BlockSpec((tm, tn), lambda i, j: (i, j)),
        out_shape=jax.ShapeDtypeStruct((M, N), out_dtype),
        compiler_params=_cparams(("parallel", "parallel"), vmem_mib),
        name=name,
    )(*args)


def _head_mm_kernel(x_ref, w_ref, o_ref, *, heads, k, n):
    for h in range(heads):
        o_ref[:, h * n:(h + 1) * n] = jnp.dot(
            x_ref[:, h * k:(h + 1) * k], w_ref[h], preferred_element_type=F32).astype(o_ref.dtype)


def head_mm(x, w, out_dtype, name):
    R = x.shape[0]
    H, K, N = w.shape
    return pl.pallas_call(
        functools.partial(_head_mm_kernel, heads=H, k=K, n=N),
        grid=(1,),
        in_specs=[pl.BlockSpec((R, H * K), lambda i: (0, 0)),
                  pl.BlockSpec((H, K, N), lambda i: (0, 0, 0))],
        out_specs=pl.BlockSpec((R, H * N), lambda i: (0, 0)),
        out_shape=jax.ShapeDtypeStruct((R, H * N), out_dtype),
        compiler_params=_cparams(("arbitrary",)),
        name=name,
    )(x, w)


def _norm_kernel(x_ref, g_ref, o_ref):
    o_ref[...] = _rms(x_ref[...], g_ref[...])


def final_norm_call(x, g, row0, n_rows, tm):
    D = x.shape[1]
    assert row0 % tm == 0 and n_rows % tm == 0
    blk0 = row0 // tm
    return pl.pallas_call(
        _norm_kernel,
        grid=(n_rows // tm,),
        in_specs=[pl.BlockSpec((tm, D), lambda i: (i + blk0, 0)),
                  pl.BlockSpec((1, D), lambda i: (0, 0))],
        out_specs=pl.BlockSpec((tm, D), lambda i: (i, 0)),
        out_shape=jax.ShapeDtypeStruct((n_rows, D), F32),
        compiler_params=_cparams(("parallel",)),
        name="final_norm",
    )(x, g.reshape(1, D))


def _swiglu_epilogue(gate, up):
    return _silu(gate) * up


def ffn_half_step(x, g, wi, wo, layer):
    d_ff = wo.shape[1]
    hid = norm_mm(x, g, wi, (0, d_ff), d_ff, 512, _swiglu_epilogue, BF16, "ffn_in", layer=layer)
    return mm(hid, wo, F32, "ffn_out", res=x, scale=0.5, layer=layer, tm_cap=832, tn=512, vmem_mib=56)


def _mla_in_kernel(x_ref, g_ref, w_ref, gq_ref, gkv_ref, cos_ref, sin_ref,
                   cq_ref, ckv_ref, ckvb_ref, kr_ref, krb_ref):
    h = _rms(x_ref[...], g_ref[...]).astype(BF16)
    acc = jnp.dot(h, w_ref[...], preferred_element_type=F32)
    ql, kl = A_Q_LORA, A_KV_LORA
    cq_ref[...] = _rms(acc[:, :ql], gq_ref[...]).astype(BF16)
    ckv = _rms(acc[:, ql:ql + kl], gkv_ref[...])
    ckv_ref[...] = ckv
    ckvb_ref[...] = ckv.astype(BF16)
    kr = (acc[:, ql + kl:ql + kl + 128] * cos_ref[...]
          + acc[:, ql + kl + 128:ql + kl + 256] * sin_ref[...])
    kr_ref[...] = kr
    krb_ref[...] = kr.astype(BF16)


def mla_in(x, g, w_ext, g_q, g_kv, cos_t, sin_t):
    M, D = x.shape
    tm = _row_tile(M, 640)
    NW = w_ext.shape[1]
    row = lambda n: pl.BlockSpec((tm, n), lambda i: (i, 0))
    full = lambda a, b: pl.BlockSpec((a, b), lambda i: (0, 0))
    return pl.pallas_call(
        _mla_in_kernel,
        grid=(M // tm,),
        in_specs=[row(D), full(1, D), full(D, NW), full(1, A_Q_LORA), full(1, A_KV_LORA),
                  row(128), row(128)],
        out_specs=[row(A_Q_LORA), row(A_KV_LORA), row(A_KV_LORA), row(128), row(128)],
        out_shape=[jax.ShapeDtypeStruct((M, A_Q_LORA), BF16),
                   jax.ShapeDtypeStruct((M, A_KV_LORA), F32),
                   jax.ShapeDtypeStruct((M, A_KV_LORA), BF16),
                   jax.ShapeDtypeStruct((M, 128), F32),
                   jax.ShapeDtypeStruct((M, 128), BF16)],
        compiler_params=_cparams(("parallel",)),
        name="mla_in",
    )(x, g.reshape(1, D), w_ext, g_q.reshape(1, -1), g_kv.reshape(1, -1), cos_t, sin_t)


def _mla_q_kernel(cq_ref, w1_ref, w2_ref, cos_ref, sin_ref, o_ref):
    cq = cq_ref[...]
    cos = cos_ref[...]
    sin = sin_ref[...]
    P = A_QK_PAD
    for h in range(A_HEADS):
        a1 = jnp.dot(cq, w1_ref[:, h * P:(h + 1) * P], preferred_element_type=F32)
        a2 = jnp.dot(cq, w2_ref[:, h * 128:(h + 1) * 128], preferred_element_type=F32)
        o_ref[:, h * P:h * P + 128] = a1[:, :128].astype(BF16)
        o_ref[:, h * P + 128:(h + 1) * P] = (a1[:, 128:] * cos + a2 * sin).astype(BF16)


def mla_q(cq, w1, w2, cos_t, sin_t):
    M = cq.shape[0]
    tm = _row_tile(M, 640)
    row = lambda n: pl.BlockSpec((tm, n), lambda i: (i, 0))
    full = lambda a, b: pl.BlockSpec((a, b), lambda i: (0, 0))
    return pl.pallas_call(
        _mla_q_kernel,
        grid=(M // tm,),
        in_specs=[row(A_Q_LORA), full(*w1.shape), full(*w2.shape), row(128), row(128)],
        out_specs=row(A_HEADS * A_QK_PAD),
        out_shape=jax.ShapeDtypeStruct((M, A_HEADS * A_QK_PAD), BF16),
        compiler_params=_cparams(("parallel",)),
        name="mla_q",
    )(cq, w1, w2, cos_t, sin_t)


def _mla_kv_kernel(c_ref, wk_ref, wvt_ref, kn_ref, vt_ref):
    c = c_ref[...]
    kn_ref[...] = jnp.dot(c, wk_ref[...], preferred_element_type=F32).astype(BF16)
    vt_ref[...] = lax.dot_general(wvt_ref[...], c, (((1,), (1,)), ((), ())),
                                  preferred_element_type=F32).astype(BF16)


def mla_kv(ckvb, w_uk_cols, w_uv_rows, n_rows, tm):
    C = A_KV_LORA
    N = w_uk_cols.shape[1]
    return pl.pallas_call(
        _mla_kv_kernel,
        grid=(n_rows // tm,),
        in_specs=[pl.BlockSpec((tm, C), lambda i: (i, 0)),
                  pl.BlockSpec((C, N), lambda i: (0, 0)),
                  pl.BlockSpec((N, C), lambda i: (0, 0))],
        out_specs=[pl.BlockSpec((tm, N), lambda i: (i, 0)),
                   pl.BlockSpec((N, tm), lambda i: (0, i))],
        out_shape=[jax.ShapeDtypeStruct((n_rows, N), BF16),
                   jax.ShapeDtypeStruct((N, n_rows), BF16)],
        compiler_params=_cparams(("parallel",)),
        name="mla_kv",
    )(ckvb, w_uk_cols, w_uv_rows)


def _flash_kernel(q_ref, kn_ref, kr_ref, vt_ref, o_ref, m_ref, l_ref, acc_ref, *, scale, tile, n_tiles):
    nt_dims = (((1,), (1,)), ((), ()))

    def block(qi, ki, masked):
        q0 = pl.multiple_of(qi * tile, tile)
        k0 = pl.multiple_of(ki * tile, tile)
        q = q_ref[pl.ds(q0, tile), :]
        k = jnp.concatenate([kn_ref[pl.ds(k0, tile), :], kr_ref[pl.ds(k0, tile), :]], axis=1)
        st = lax.dot_general(k, q, nt_dims, preferred_element_type=F32) * scale
        if masked:
            kpos = lax.broadcasted_iota(jnp.int32, st.shape, 0)
            qpos = lax.broadcasted_iota(jnp.int32, st.shape, 1)
            st = jnp.where(kpos <= qpos, st, -jnp.inf)
        m_prev = m_ref[...]
        m_new = jnp.maximum(m_prev, jnp.max(st, axis=0, keepdims=True))
        alpha = jnp.exp(m_prev - m_new)
        p = jnp.exp(st - m_new)
        l_ref[...] = alpha * l_ref[...] + jnp.sum(p, axis=0, keepdims=True)
        acc_ref[...] = alpha * acc_ref[...] + jnp.dot(vt_ref[:, pl.ds(k0, tile)], p.astype(BF16),
                                                       preferred_element_type=F32)
        m_ref[...] = m_new

    def q_tile(qi, carry):
        m_ref[...] = jnp.full(m_ref.shape, -jnp.inf, F32)
        l_ref[...] = jnp.zeros(l_ref.shape, F32)
        acc_ref[...] = jnp.zeros(acc_ref.shape, F32)

        def k_tile(ki, c):
            block(qi, ki, False)
            return c

        lax.fori_loop(0, qi, k_tile, 0)
        block(qi, qi, True)
        q0 = pl.multiple_of(qi * tile, tile)
        o_ref[pl.ds(q0, tile), :] = (acc_ref[...] / l_ref[...]).T.astype(o_ref.dtype)
        return carry

    lax.fori_loop(0, n_tiles, q_tile, 0)


def mla_flash(qp, kn, krb, vt, n_batch, seq, tile):
    H = A_HEADS
    scale = (A_NOPE + A_ROPE) ** -0.5
    return pl.pallas_call(
        functools.partial(_flash_kernel, scale=scale, tile=tile, n_tiles=seq // tile),
        grid=(n_batch, H),
        in_specs=[pl.BlockSpec((seq, A_QK_PAD), lambda b, h: (b, h)),
                  pl.BlockSpec((seq, A_NOPE), lambda b, h: (b, h)),
                  pl.BlockSpec((seq, 128), lambda b, h: (b, 0)),
                  pl.BlockSpec((A_VDIM, seq), lambda b, h: (h, b))],
        out_specs=pl.BlockSpec((seq, A_VDIM), lambda b, h: (b, h)),
        out_shape=jax.ShapeDtypeStruct((n_batch * seq, H * A_VDIM), BF16),
        scratch_shapes=[pltpu.VMEM((1, tile), F32), pltpu.VMEM((1, tile), F32),
                        pltpu.VMEM((A_VDIM, tile), F32)],
        compiler_params=_cparams(("parallel", "parallel")),
        name="mla_flash",
    )(qp, kn, krb, vt)


def _decode_kernel(pt_ref, ql_ref, qr_ref, cn_ref, rn_ref, ckv_hbm, krt_hbm, o_ref,
                   cbuf, rbuf, sems, m_ref, l_ref, acc_ref,
                   *, layer, n_pages, chunk, group, scale):
    g = pl.program_id(0)
    n_steps = pl.num_programs(0)
    n_chunks = n_pages // chunk
    c = g % n_chunks
    slot = g % 2

    def copies(step, slot_):
        out = []
        rg = step // n_chunks
        cc = step % n_chunks
        for r in range(group):
            base = (rg * group + r) * n_pages + cc * chunk
            for p in range(chunk):
                page = pt_ref[base + p]
                out.append(pltpu.make_async_copy(ckv_hbm.at[layer, page], cbuf.at[slot_, r, p],
                                                 sems.at[0, slot_, r, p]))
                out.append(pltpu.make_async_copy(krt_hbm.at[layer, page], rbuf.at[slot_, r, p],
                                                 sems.at[1, slot_, r, p]))
        return out

    @pl.when(g == 0)
    def _():
        for cp in copies(0, 0):
            cp.start()

    @pl.when(g + 1 < n_steps)
    def _():
        for cp in copies(g + 1, 1 - slot):
            cp.start()

    @pl.when(c == 0)
    def _():
        for r in range(group):
            cn = cn_ref[r].astype(F32)
            rn = rn_ref[r].astype(F32)
            s_new = (jnp.sum(ql_ref[r].astype(F32) * cn, axis=-1, keepdims=True)
                     + jnp.sum(qr_ref[r].astype(F32) * rn, axis=-1, keepdims=True)) * scale
            m_ref[r] = s_new
            l_ref[r] = jnp.ones(s_new.shape, F32)
            acc_ref[r] = jnp.broadcast_to(cn, acc_ref.shape[1:])

    for cp in copies(g, slot):
        cp.wait()

    nt_dims = (((1,), (1,)), ((), ()))
    for r in range(group):
        ql = ql_ref[r]
        qr = qr_ref[r]
        cb = cbuf[slot, r].reshape(chunk * PAGE_SIZE, A_KV_LORA).astype(BF16)
        s_rope = jnp.concatenate(
            [jnp.dot(qr, rbuf[slot, r, p].astype(BF16), preferred_element_type=F32)
             for p in range(chunk)], axis=1)
        s = (lax.dot_general(ql, cb, nt_dims, preferred_element_type=F32) + s_rope) * scale
        m_prev = m_ref[r]
        m_new = jnp.maximum(m_prev, jnp.max(s, axis=-1, keepdims=True))
        alpha = jnp.exp(m_prev - m_new)
        p = jnp.exp(s - m_new)
        l_ref[r] = alpha * l_ref[r] + jnp.sum(p, axis=-1, keepdims=True)
        acc_ref[r] = alpha * acc_ref[r] + jnp.dot(p.astype(BF16), cb, preferred_element_type=F32)
        m_ref[r] = m_new

    @pl.when(c == n_chunks - 1)
    def _():
        for r in range(group):
            o_ref[r] = (acc_ref[r] / l_ref[r]).astype(o_ref.dtype)


def mla_decode(page_table, ql, qr, cn, rn, cache_ckv, cache_krt, layer, chunk=8, group=2):
    Bd, n_pages = page_table.shape
    assert n_pages % chunk == 0 and Bd % group == 0
    n_chunks = n_pages // chunk
    H, C, R = A_HEADS, A_KV_LORA, A_ROPE
    scale = (A_NOPE + A_ROPE) ** -0.5
    req = lambda shp: pl.BlockSpec((group,) + shp, lambda g, pt: (g // n_chunks, 0, 0))
    grid_spec = pltpu.PrefetchScalarGridSpec(
        num_scalar_prefetch=1,
        grid=(Bd // group * n_chunks,),
        in_specs=[req((H, C)), req((H, R)), req((1, C)), req((1, R)),
                  pl.BlockSpec(memory_space=pl.ANY), pl.BlockSpec(memory_space=pl.ANY)],
        out_specs=req((H, C)),
        scratch_shapes=[pltpu.VMEM((2, group, chunk, PAGE_SIZE, C), F32),
                        pltpu.VMEM((2, group, chunk, R, PAGE_SIZE), F32),
                        pltpu.SemaphoreType.DMA((2, 2, group, chunk)),
                        pltpu.VMEM((group, H, 1), F32), pltpu.VMEM((group, H, 1), F32),
                        pltpu.VMEM((group, H, C), F32)])
    return pl.pallas_call(
        functools.partial(_decode_kernel, layer=layer, n_pages=n_pages, chunk=chunk, group=group,
                          scale=scale),
        grid_spec=grid_spec,
        out_shape=jax.ShapeDtypeStruct((Bd, H, C), BF16),
        compiler_params=_cparams(("arbitrary",)),
        name="mla_decode",
    )(page_table.reshape(-1), ql, qr, cn, rn, cache_ckv, cache_krt)


def _rope_tables(pos):
    half = A_ROPE // 2
    inv = ROPE_THETA ** (-jnp.arange(half, dtype=F32) / half)
    ang = pos.astype(F32)[:, None] * inv[None, :]
    cos, sin = jnp.cos(ang), jnp.sin(ang)
    z = jnp.zeros((pos.shape[0], 128 - A_ROPE), F32)
    return (jnp.concatenate([cos, cos, z], axis=1), jnp.concatenate([sin, sin, z], axis=1))


def _half_swap(w):
    half = A_ROPE // 2
    return jnp.concatenate([-w[..., half:], w[..., :half]], axis=-1)


def mla_layer(x, g_mix, w_in, g_q, g_kv, w_uq, w_ukv, w_o_all, cache_ckv, cache_krt, page_table,
              layer, n_batch, seq, cos_t, sin_t, flash_tile):
    D = x.shape[1]
    Mp = n_batch * seq
    H = A_HEADS
    lo = A_Q_LORA + A_KV_LORA
    zpad = jnp.zeros((D, 128 - A_ROPE), F32)
    w_kr = w_in[:, lo:]
    w_in_ext = jnp.concatenate([w_in[:, :lo], w_kr, zpad, _half_swap(w_kr), zpad], axis=1).astype(BF16)
    cq, ckv, ckvb, kr, krb = mla_in(x, g_mix, w_in_ext, g_q, g_kv, cos_t, sin_t)

    wq = w_uq.reshape(A_Q_LORA, H, A_NOPE + A_ROPE)
    wq_rope = wq[..., A_NOPE:]
    zq = jnp.zeros((A_Q_LORA, H, 128 - A_ROPE), F32)
    w1 = jnp.concatenate([wq[..., :A_NOPE], wq_rope, zq], axis=-1).reshape(A_Q_LORA, H * A_QK_PAD)
    w2 = jnp.concatenate([_half_swap(wq_rope), zq], axis=-1).reshape(A_Q_LORA, H * 128)
    qp = mla_q(cq, w1.astype(BF16), w2.astype(BF16), cos_t, sin_t)

    wkv = w_ukv.reshape(A_KV_LORA, H, A_NOPE + A_VDIM)
    w_uk = wkv[..., :A_NOPE]
    w_uv = wkv[..., A_NOPE:]
    kn, vt = mla_kv(ckvb, w_uk.reshape(A_KV_LORA, H * A_NOPE).astype(BF16),
                    jnp.transpose(w_uv, (1, 2, 0)).reshape(H * A_VDIM, A_KV_LORA).astype(BF16),
                    Mp, min(1024, seq))
    o_p = mla_flash(qp, kn, krb, vt, n_batch, seq, flash_tile)

    Bd = page_table.shape[0]
    qs = qp[Mp:].reshape(Bd, H, A_QK_PAD)
    q_nope = qs[:, :, :A_NOPE].reshape(Bd, H * A_NOPE)
    q_rope = qs[:, :, A_NOPE:A_NOPE + A_ROPE]
    q_lat = head_mm(q_nope, jnp.transpose(w_uk, (1, 2, 0)).astype(BF16), BF16, "mla_qlat")
    o_lat = mla_decode(page_table, q_lat.reshape(Bd, H, A_KV_LORA), q_rope,
                       ckvb[Mp:].reshape(Bd, 1, A_KV_LORA),
                       krb[Mp:, :A_ROPE].reshape(Bd, 1, A_ROPE),
                       cache_ckv, cache_krt, layer)
    o_s = head_mm(o_lat.reshape(Bd, H * A_KV_LORA), jnp.transpose(w_uv, (1, 0, 2)).astype(BF16), BF16,
                  "mla_ouv")

    o = jnp.concatenate([o_p, o_s], axis=0)
    x = mm(o, w_o_all, F32, "mla_out", res=x, scale=1.0, layer=layer, tn=512)
    return x, ckv, kr[:, :A_ROPE]


def _lower_bound(raw, layer):
    e = jnp.exp(raw - jnp.max(raw, axis=0, keepdims=True))
    tot = jnp.sum(e, axis=0, keepdims=True)
    if layer == 0:
        return jnp.zeros_like(tot)
    return jnp.sum(e[1:layer + 1], axis=0, keepdims=True) / tot


def _hgrn_scan_kernel(q_ref, f_ref, i_ref, og_ref, lbr_ref, go_ref, o_ref, st_ref,
                      cum_ref, state_ref, qs_ref, k_ref, *, layer, tb, hb):
    t = pl.program_id(2)

    @pl.when(t == 0)
    def _():
        state_ref[...] = jnp.zeros(state_ref.shape, F32)

    lb = _lower_bound(lbr_ref[...], layer)
    gate = lb + (1.0 - lb) * _sigmoid(f_ref[...])
    logf = jnp.log(gate)
    r_id = lax.broadcasted_iota(jnp.int32, (tb, tb), 0)
    c_id = lax.broadcasted_iota(jnp.int32, (tb, tb), 1)
    tri = jnp.where(r_id >= c_id, 1.0, 0.0).astype(BF16)
    a1 = logf.astype(BF16)
    r1 = logf - a1.astype(F32)
    a2 = r1.astype(BF16)
    a3 = (r1 - a2.astype(F32)).astype(BF16)
    cum = (jnp.dot(tri, a1, preferred_element_type=F32)
           + jnp.dot(tri, a2, preferred_element_type=F32)
           + jnp.dot(tri, a3, preferred_element_type=F32))
    for hh in range(hb):
        cum_ref[hh, 0:8, :] = jnp.zeros((8, B_DK), F32)
        cum_ref[hh, 8:8 + tb, :] = cum[:, hh * B_DK:(hh + 1) * B_DK]
    qs_ref[...] = _silu(q_ref[...])
    k_ref[...] = 1.0 - gate

    rows = lax.broadcasted_iota(jnp.int32, (B_SUB, 1), 0)

    def sub_chunk(c, carry):
        r0 = pl.multiple_of(c * B_SUB, B_SUB)
        for hh in range(hb):
            ls = slice(hh * B_DK, (hh + 1) * B_DK)
            b = cum_ref[hh, pl.ds(r0 + 8, B_SUB), :] - cum_ref[hh, pl.ds(r0 + 7, 1), :]
            q = qs_ref[pl.ds(r0, B_SUB), ls]
            k = k_ref[pl.ds(r0, B_SUB), ls]
            v = i_ref[pl.ds(r0, B_SUB), ls]
            st = state_ref[hh]
            b_last = b[B_SUB - 1:B_SUB, :]
            o = lax.dot_general((q * jnp.exp(b)).astype(BF16), st.astype(BF16),
                                (((1,), (1,)), ((), ())), preferred_element_type=F32)
            kd = (k * jnp.exp(b_last - b)).astype(BF16)
            inc = lax.dot_general(v.astype(BF16), kd, (((0,), (0,)), ((), ())),
                                  preferred_element_type=F32)
            state_ref[hh] = st * jnp.exp(b_last) + inc
            for s in range(B_SUB):
                d = jnp.where(rows >= s, b - b[s:s + 1, :], -1e30)
                a_col = jnp.sum(q * jnp.exp(d) * k[s:s + 1, :], axis=-1, keepdims=True)
                o = o + a_col * v[s:s + 1, :]
            o = _rms(o, go_ref[:, ls]) * _silu(og_ref[pl.ds(r0, B_SUB), ls])
            o_ref[pl.ds(r0, B_SUB), ls] = o.astype(o_ref.dtype)
        return carry

    lax.fori_loop(0, tb // B_SUB, sub_chunk, 0)

    @pl.when(t == pl.num_programs(2) - 1)
    def _():
        for hh in range(hb):
            st_ref[0, hh] = state_ref[hh].T


def hgrn_scan(proj, lb_raw, g_o, layer, n_batch, seq, tb, hb=4):
    H = B_HEADS
    assert H % hb == 0
    nt = seq // tb
    W = hb * B_DK
    sec = lambda k: pl.BlockSpec((tb, W), lambda b, h, t, k=k: (b * nt + t, k * (H // hb) + h))
    return pl.pallas_call(
        functools.partial(_hgrn_scan_kernel, layer=layer, tb=tb, hb=hb),
        grid=(n_batch, H // hb, nt),
        in_specs=[sec(0), sec(1), sec(2), sec(3),
                  pl.BlockSpec((lb_raw.shape[0], W), lambda b, h, t: (0, h)),
                  pl.BlockSpec((1, W), lambda b, h, t: (0, h))],
        out_specs=[pl.BlockSpec((tb, W), lambda b, h, t: (b * nt + t, h)),
                   pl.BlockSpec((1, hb, B_DK, B_DV), lambda b, h, t: (b, h, 0, 0))],
        out_shape=[jax.ShapeDtypeStruct((n_batch * seq, H * B_DV), BF16),
                   jax.ShapeDtypeStruct((n_batch, H, B_DK, B_DV), F32)],
        scratch_shapes=[pltpu.VMEM((hb, 8 + tb, B_DK), F32), pltpu.VMEM((hb, B_DV, B_DK), F32),
                        pltpu.VMEM((tb, W), F32), pltpu.VMEM((tb, W), F32)],
        compiler_params=_cparams(("parallel", "parallel", "arbitrary")),
        name="hgrn_scan",
    )(proj, proj, proj, proj, lb_raw, g_o.reshape(1, -1))


def _hgrn_step_kernel(p_ref, s_ref, lbr_ref, go_ref, o_ref, so_ref, *, layer):
    H = B_HEADS
    p = p_ref[0]
    lb = _lower_bound(lbr_ref[...], layer)[0]
    gate = lb + (1.0 - lb) * _sigmoid(p[H:2 * H])
    q = _silu(p[0:H])
    v = p[2 * H:3 * H]
    og = p[3 * H:4 * H]
    packed = jnp.concatenate([gate, q, jnp.zeros((128 - 2 * H, B_DK), F32)], axis=0)
    cols = packed.T
    outs = []
    for h in range(H):
        g_col = cols[:, h:h + 1]
        q_col = cols[:, H + h:H + h + 1]
        s_new = g_col * s_ref[0, h] + (1.0 - g_col) * v[h:h + 1, :]
        so_ref[0, h] = s_new
        outs.append(jnp.sum(q_col * s_new, axis=0, keepdims=True))
    o = jnp.concatenate(outs, axis=0)
    o_ref[0] = (_rms(o, go_ref[...]) * _silu(og)).astype(o_ref.dtype)


def hgrn_step(proj_s, state, lb_raw, g_o, layer):
    Bd = proj_s.shape[0]
    H = B_HEADS
    depth = lb_raw.shape[0]
    o, s_new = pl.pallas_call(
        functools.partial(_hgrn_step_kernel, layer=layer),
        grid=(Bd,),
        in_specs=[pl.BlockSpec((1, 4 * H, 128), lambda r: (r, 0, 0)),
                  pl.BlockSpec((1, H, B_DK, B_DV), lambda r: (r, 0, 0, 0)),
                  pl.BlockSpec((depth, H, B_DK), lambda r: (0, 0, 0)),
                  pl.BlockSpec((H, B_DV), lambda r: (0, 0))],
        out_specs=[pl.BlockSpec((1, H, B_DV), lambda r: (r, 0, 0)),
                   pl.BlockSpec((1, H, B_DK, B_DV), lambda r: (r, 0, 0, 0))],
        out_shape=[jax.ShapeDtypeStruct((Bd, H, B_DV), BF16),
                   jax.ShapeDtypeStruct(state.shape, F32)],
        compiler_params=_cparams(("parallel",)),
        name="hgrn_step",
    )(proj_s.reshape(Bd, 4 * H, 128), state, lb_raw.reshape(depth, H, B_DK), g_o.reshape(H, B_DV))
    return o.reshape(Bd, H * B_DV), s_new


def hgrn_layer(x, g_mix, w_in_all, lb_raw, g_o, w_o_all, state, j, layer, n_batch, seq, scan_tb):
    Mp = n_batch * seq
    proj = norm_mm(x, g_mix, w_in_all, (0,), w_in_all.shape[-1], 512, lambda a: a, F32, "hgrn_in", layer=j)
    o_p, st_p = hgrn_scan(proj, lb_raw, g_o, layer, n_batch, seq, scan_tb)
    o_s, st_s = hgrn_step(proj[Mp:], state, lb_raw, g_o, layer)
    o = jnp.concatenate([o_p, o_s], axis=0)
    x = mm(o, w_o_all, F32, "hgrn_out", res=x, scale=1.0, layer=j, tn=512)
    return x, st_p, st_s


def _gmlp_mix_kernel(u_ref, v_ref, gv_ref, ws_ref, bias_ref, vo_ref, z_ref):
    v = _rms(v_ref[...], gv_ref[...])
    vo_ref[...] = v
    vb = v.astype(BF16)
    L = v.shape[0]
    r_id = lax.broadcasted_iota(jnp.int32, (L, L), 0)
    c_id = lax.broadcasted_iota(jnp.int32, (L, L), 1)
    for g in range(C_GROUPS):
        sl = slice(g * C_GDIM, (g + 1) * C_GDIM)
        w = jnp.where(r_id >= c_id, ws_ref[g], 0.0).astype(BF16)
        mixed = jnp.dot(w, vb[:, sl], preferred_element_type=F32) + bias_ref[:, sl]
        z_ref[:, sl] = (u_ref[:, sl] * mixed).astype(z_ref.dtype)


def gmlp_mix(uv, g_v, w_s, bias_full, n_rows):
    W = g_v.shape[0]
    L = C_CHUNK
    return pl.pallas_call(
        _gmlp_mix_kernel,
        grid=(n_rows // L,),
        in_specs=[pl.BlockSpec((L, W), lambda c: (c, 0)),
                  pl.BlockSpec((L, W), lambda c: (c, 1)),
                  pl.BlockSpec((1, W), lambda c: (0, 0)),
                  pl.BlockSpec((C_GROUPS, L, L), lambda c: (0, 0, 0)),
                  pl.BlockSpec((L, W), lambda c: (0, 0))],
        out_specs=[pl.BlockSpec((L, W), lambda c: (c, 0)),
                   pl.BlockSpec((L, W), lambda c: (c, 0))],
        out_shape=[jax.ShapeDtypeStruct((n_rows, W), F32),
                   jax.ShapeDtypeStruct((n_rows, W), BF16)],
        compiler_params=_cparams(("parallel",)),
        name="gmlp_mix",
    )(uv, uv, g_v.reshape(1, W), w_s, bias_full)


def _gmlp_single_kernel(u_ref, v_ref, gv_ref, w0_ref, b0_ref, vo_ref, z_ref):
    v = _rms(v_ref[...], gv_ref[...])
    vo_ref[...] = v
    z_ref[...] = (u_ref[...] * (w0_ref[...] * v + b0_ref[...])).astype(z_ref.dtype)


def gmlp_single(uv_s, g_v, w0, b0):
    R = uv_s.shape[0]
    W = g_v.shape[0]
    blk = lambda j: pl.BlockSpec((R, W), lambda i, j=j: (0, j))
    vec = pl.BlockSpec((1, W), lambda i: (0, 0))
    return pl.pallas_call(
        _gmlp_single_kernel,
        grid=(1,),
        in_specs=[blk(0), blk(1), vec, vec, vec],
        out_specs=[blk(0), blk(0)],
        out_shape=[jax.ShapeDtypeStruct((R, W), F32), jax.ShapeDtypeStruct((R, W), BF16)],
        compiler_params=_cparams(("arbitrary",)),
        name="gmlp_single",
    )(uv_s, uv_s, g_v.reshape(1, W), w0, b0)


def gmlp_layer(x, g_mix, w_in_all, g_v, w_s, b_s, w_o_all, j, n_batch, seq):
    Mp = n_batch * seq
    W = g_v.shape[0]
    uv = norm_mm(x, g_mix, w_in_all, (0,), w_in_all.shape[-1], 512, _gelu_tanh, F32, "gmlp_in", layer=j)
    assert seq % C_CHUNK == 0
    bias_full = jnp.repeat(b_s[:, :C_CHUNK].T, C_GDIM, axis=1)
    v_p, z_p = gmlp_mix(uv, g_v, w_s[:, :C_CHUNK, :C_CHUNK], bias_full, Mp)
    w0 = jnp.repeat(w_s[:, 0, 0], C_GDIM).reshape(1, W)
    b0 = jnp.repeat(b_s[:, 0], C_GDIM).reshape(1, W)
    v_s, z_s = gmlp_single(uv[Mp:], g_v, w0, b0)
    z = jnp.concatenate([z_p, z_s], axis=0)
    x = mm(z, w_o_all, F32, "gmlp_out", res=x, scale=1.0, layer=j, tn=512)
    last = ((seq - 1) // C_CHUNK) * C_CHUNK
    v_last = v_p.reshape(n_batch, seq, W)[:, last:]
    return x, v_last, v_s


def _forward(x_prompt, x_sample, cache_a_ckv, cache_a_kr, state_b, page_table, norm_ffn_a,
             ffn_a_wi, ffn_a_wo, norm_mix, a_w_in, a_g_q, a_g_kv, a_w_uq, a_w_ukv, a_w_o,
             b_w_in, b_lower_bounds, b_g_o, b_w_o, c_w_in, c_g_v, c_w_s, c_b_s, c_w_o,
             norm_ffn_b, ffn_b_wi, ffn_b_wo, final_norm, *, flash_tile, scan_tb):
    n_batch, seq, D = x_prompt.shape
    Bd, dec_seq, _ = x_sample.shape
    assert dec_seq == 1
    depth = norm_mix.shape[0]
    Mp = n_batch * seq
    past_len = page_table.shape[1] * PAGE_SIZE
    x = jnp.concatenate([x_prompt.reshape(Mp, D), x_sample.reshape(Bd, D)], axis=0)

    pos = jnp.concatenate([jnp.tile(jnp.arange(seq), n_batch),
                           jnp.full((Bd,), past_len, jnp.int32)])
    cos_t, sin_t = _rope_tables(pos)

    cache_krt = jnp.swapaxes(cache_a_kr, 2, 3)

    a_ckv, a_kr, b_sp, b_ss, c_vp, c_vs = [], [], [], [], [], []
    for i in range(depth):
        x = ffn_half_step(x, norm_ffn_a[i], ffn_a_wi, ffn_a_wo, i)
        j = i // N_MIXERS
        kind = i % N_MIXERS
        if kind == 0:
            x, ckv, kr = mla_layer(x, norm_mix[i], a_w_in[j], a_g_q[j], a_g_kv[j], a_w_uq[j],
                                   a_w_ukv[j], a_w_o, cache_a_ckv, cache_krt, page_table,
                                   j, n_batch, seq, cos_t, sin_t, flash_tile)
            a_ckv.append(ckv)
            a_kr.append(kr)
        elif kind == 1:
            x, sp, ss = hgrn_layer(x, norm_mix[i], b_w_in, b_lower_bounds, b_g_o[j], b_w_o,
                                   state_b[j], j, i, n_batch, seq, scan_tb)
            b_sp.append(sp)
            b_ss.append(ss)
        else:
            x, vp, vs = gmlp_layer(x, norm_mix[i], c_w_in, c_g_v[j], c_w_s[j], c_b_s[j],
                                   c_w_o, j, n_batch, seq)
            c_vp.append(vp)
            c_vs.append(vs)
        x = ffn_half_step(x, norm_ffn_b[i], ffn_b_wi, ffn_b_wo, i)

    y_p = final_norm_call(x, final_norm, 0, Mp, min(1024, Mp))
    y_s = final_norm_call(x, final_norm, Mp, Bd, Bd)
    ckv_all = jnp.stack(a_ckv)
    kr_all = jnp.stack(a_kr)
    return (y_p.reshape(n_batch, seq, D), y_s.reshape(Bd, 1, D),
            ckv_all[:, :Mp].reshape(-1, n_batch, seq, A_KV_LORA),
            kr_all[:, :Mp].reshape(-1, n_batch, seq, A_ROPE),
            ckv_all[:, Mp:].reshape(-1, Bd, 1, A_KV_LORA),
            kr_all[:, Mp:].reshape(-1, Bd, 1, A_ROPE),
            jnp.stack(b_sp), jnp.stack(b_ss),
            jnp.stack(c_vp), jnp.stack(c_vs)[:, :, None, :])


def kernel(x_prompt, x_sample, cache_a_ckv, cache_a_kr, state_b, page_table, norm_ffn_a, ffn_a_wi, ffn_a_wo, norm_mix, a_w_in, a_g_q, a_g_kv, a_w_uq, a_w_ukv, a_w_o, b_w_in, b_lower_bounds, b_g_o, b_w_o, c_w_in, c_g_v, c_w_s, c_b_s, c_w_o, norm_ffn_b, ffn_b_wi, ffn_b_wo, final_norm):
    seq = x_prompt.shape[1]
    return _forward(x_prompt, x_sample, cache_a_ckv, cache_a_kr, state_b, page_table, norm_ffn_a,
                    ffn_a_wi, ffn_a_wo, norm_mix, a_w_in, a_g_q, a_g_kv, a_w_uq, a_w_ukv, a_w_o,
                    b_w_in, b_lower_bounds, b_g_o, b_w_o, c_w_in, c_g_v, c_w_s, c_b_s, c_w_o,
                    norm_ffn_b, ffn_b_wi, ffn_b_wo, final_norm,
                    flash_tile=min(1024, seq), scan_tb=min(256, seq))
```

```python
import functools
import math

import jax
import jax.numpy as jnp
from jax import lax
from jax.experimental import pallas as pl
from jax.experimental.pallas import tpu as pltpu

F32 = jnp.float32
BF16 = jnp.bfloat16

EPS = 1e-6
ROPE_THETA = 10000.0
N_MIXERS = 3

A_HEADS = 16
A_NOPE = 128
A_ROPE = 64
A_VDIM = 128
A_Q_LORA = 512
A_KV_LORA = 512
A_QK_PAD = 256
Q_PRESCALE = (A_NOPE + A_ROPE) ** -0.5 * math.log2(math.e)
PAGE_SIZE = 128

B_HEADS = 16
B_DK = 128
B_DV = 128

C_GROUPS = 16
C_GDIM = 128
C_CHUNK = 128

MIB = 1024 * 1024


def _cparams(semantics, vmem_mib=48):
    return pltpu.CompilerParams(dimension_semantics=semantics,
                                vmem_limit_bytes=vmem_mib * MIB)


def _rms(x, g):
    return x * lax.rsqrt(jnp.mean(x * x, axis=-1, keepdims=True) + EPS) * g


def _sigmoid(x):
    return 1.0 / (1.0 + jnp.exp(-x))


def _silu(x):
    return x * _sigmoid(x)


def _gelu_tanh(x):
    c = math.sqrt(2.0 / math.pi)
    return 0.5 * x * (1.0 + jnp.tanh(c * (x + 0.044715 * (x * x * x))))


def _row_tile(m, cap):
    best = None
    for t in range(16, cap + 1, 16):
        if m % t == 0:
            best = t
    assert best is not None, (m, cap)
    return best


def _norm_mm_kernel(x_ref, g_ref, *refs, n_w, epilogue):
    w_refs = refs[:n_w]
    o_ref = refs[n_w]
    h_ref = refs[n_w + 1]

    @pl.when(pl.program_id(1) == 0)
    def _():
        h_ref[...] = _rms(x_ref[...], g_ref[...]).astype(BF16)

    h = h_ref[...]
    accs = [jnp.dot(h, w[...].astype(BF16), preferred_element_type=F32) for w in w_refs]
    o_ref[...] = epilogue(*accs).astype(o_ref.dtype)


def _weight_spec(w, layer, rows, tn, col_block):
    if layer is None:
        assert w.ndim == 2
        return pl.BlockSpec((rows, tn), lambda i, j: (0, col_block(i, j)))
    assert w.ndim == 3
    return pl.BlockSpec((None, rows, tn), lambda i, j: (layer, 0, col_block(i, j)))


def norm_mm(x, g, w, col_offsets, out_cols, tn, epilogue, out_dtype, name, layer=None, tm_cap=1040):
    M, D = x.shape
    tm = _row_tile(M, tm_cap)
    n_w = len(col_offsets)
    in_specs = [pl.BlockSpec((tm, D), lambda i, j: (i, 0)),
                pl.BlockSpec((1, D), lambda i, j: (0, 0))]
    for off in col_offsets:
        assert off % tn == 0
        in_specs.append(_weight_spec(w, layer, D, tn, lambda i, j, o=off // tn: j + o))
    return pl.pallas_call(
        functools.partial(_norm_mm_kernel, n_w=n_w, epilogue=epilogue),
        grid=(M // tm, out_cols // tn),
        in_specs=in_specs,
        out_specs=pl.BlockSpec((tm, tn), lambda i, j: (i, j)),
        out_shape=jax.ShapeDtypeStruct((M, out_cols), out_dtype),
        scratch_shapes=[pltpu.VMEM((tm, D), BF16)],
        compiler_params=_cparams(("parallel", "arbitrary")),
        name=name,
    )(x, g.reshape(1, D), *([w] * n_w))


def _mm_kernel(a_ref, w_ref, *refs, scale, has_res):
    o_ref = refs[-1]
    acc = jnp.dot(a_ref[...], w_ref[...].astype(BF16), preferred_element_type=F32)
    if has_res:
        acc = refs[0][...] + scale * acc
    o_ref[...] = acc.astype(o_ref.dtype)


def mm(a, w, out_dtype, name, res=None, scale=1.0, layer=None, tm_cap=1040, tn=512, vmem_mib=48):
    M, K = a.shape
    N = w.shape[-1]
    tm = _row_tile(M, tm_cap)
    tn = min(tn, N)
    in_specs = [pl.BlockSpec((tm, K), lambda i, j: (i, 0)),
                _weight_spec(w, layer, K, tn, lambda i, j: j)]
    args = [a, w]
    if res is not None:
        in_specs.append(pl.BlockSpec((tm, tn), lambda i, j: (i, j)))
        args.append(res)
    return pl.pallas_call(
        functools.partial(_mm_kernel, scale=scale, has_res=res is not None),
        grid=(M // tm, N // tn),
        in_specs=in_specs,
        out_specs=pl.BlockSpec((tm, tn), lambda i, j: (i, j)),
        out_shape=jax.ShapeDtypeStruct((M, N), out_dtype),
        compiler_params=_cparams(("parallel", "parallel"), vmem_mib),
        name=name,
    )(*args)


def _head_mm_kernel(x_ref, w_ref, o_ref, *, heads, k, n):
    for h in range(heads):
        o_ref[:, h * n:(h + 1) * n] = jnp.dot(
            x_ref[:, h * k:(h + 1) * k], w_ref[h], preferred_element_type=F32).astype(o_ref.dtype)


def head_mm(x, w, out_dtype, name):
    R = x.shape[0]
    H, K, N = w.shape
    return pl.pallas_call(
        functools.partial(_head_mm_kernel, heads=H, k=K, n=N),
        grid=(1,),
        in_specs=[pl.BlockSpec((R, H * K), lambda i: (0, 0)),
                  pl.BlockSpec((H, K, N), lambda i: (0, 0, 0))],
        out_specs=pl.BlockSpec((R, H * N), lambda i: (0, 0)),
        out_shape=jax.ShapeDtypeStruct((R, H * N), out_dtype),
        compiler_params=_cparams(("arbitrary",)),
        name=name,
    )(x, w)


def _norm_kernel(x_ref, g_ref, o_ref):
    o_ref[...] = _rms(x_ref[...], g_ref[...])


def final_norm_call(x, g, row0, n_rows, tm):
    D = x.shape[1]
    assert row0 % tm == 0 and n_rows % tm == 0
    blk0 = row0 // tm
    return pl.pallas_call(
        _norm_kernel,
        grid=(n_rows // tm,),
        in_specs=[pl.BlockSpec((tm, D), lambda i: (i + blk0, 0)),
                  pl.BlockSpec((1, D), lambda i: (0, 0))],
        out_specs=pl.BlockSpec((tm, D), lambda i: (i, 0)),
        out_shape=jax.ShapeDtypeStruct((n_rows, D), F32),
        compiler_params=_cparams(("parallel",)),
        name="final_norm",
    )(x, g.reshape(1, D))


def _swiglu_epilogue(gate, up):
    return _silu(gate) * up


def ffn_half_step(x, g, wi, wo, layer):
    d_ff = wo.shape[1]
    hid = norm_mm(x, g, wi, (0, d_ff), d_ff, 512, _swiglu_epilogue, BF16, "ffn_in", layer=layer)
    return mm(hid, wo, F32, "ffn_out", res=x, scale=0.5, layer=layer, tm_cap=832, tn=512, vmem_mib=56)


def _mla_in_kernel(x_ref, g_ref, w_ref, gq_ref, gkv_ref, cos_ref, sin_ref,
                   cq_ref, ckv_ref, ckvb_ref, kr_ref, krb_ref):
    h = _rms(x_ref[...], g_ref[...]).astype(BF16)
    acc = jnp.dot(h, w_ref[...], preferred_element_type=F32)
    ql, kl = A_Q_LORA, A_KV_LORA
    cq_ref[...] = _rms(acc[:, :ql], gq_ref[...]).astype(BF16)
    ckv = _rms(acc[:, ql:ql + kl], gkv_ref[...])
    ckv_ref[...] = ckv
    ckvb_ref[...] = ckv.astype(BF16)
    kr = (acc[:, ql + kl:ql + kl + 128] * cos_ref[...]
          + acc[:, ql + kl + 128:ql + kl + 256] * sin_ref[...])
    kr_ref[...] = kr
    krb_ref[...] = kr.astype(BF16)


def mla_in(x, g, w_ext, g_q, g_kv, cos_t, sin_t):
    M, D = x.shape
    tm = _row_tile(M, 640)
    NW = w_ext.shape[1]
    row = lambda n: pl.BlockSpec((tm, n), lambda i: (i, 0))
    full = lambda a, b: pl.BlockSpec((a, b), lambda i: (0, 0))
    return pl.pallas_call(
        _mla_in_kernel,
        grid=(M // tm,),
        in_specs=[row(D), full(1, D), full(D, NW), full(1, A_Q_LORA), full(1, A_KV_LORA),
                  row(128), row(128)],
        out_specs=[row(A_Q_LORA), row(A_KV_LORA), row(A_KV_LORA), row(128), row(128)],
        out_shape=[jax.ShapeDtypeStruct((M, A_Q_LORA), BF16),
                   jax.ShapeDtypeStruct((M, A_KV_LORA), F32),
                   jax.ShapeDtypeStruct((M, A_KV_LORA), BF16),
                   jax.ShapeDtypeStruct((M, 128), F32),
                   jax.ShapeDtypeStruct((M, 128), BF16)],
        compiler_params=_cparams(("parallel",)),
        name="mla_in",
    )(x, g.reshape(1, D), w_ext, g_q.reshape(1, -1), g_kv.reshape(1, -1), cos_t, sin_t)


def _mla_q_kernel(cq_ref, w1_ref, w2_ref, cos_ref, sin_ref, o_ref):
    cq = cq_ref[...]
    cos = cos_ref[...]
    sin = sin_ref[...]
    P = A_QK_PAD
    for h in range(A_HEADS):
        a1 = jnp.dot(cq, w1_ref[:, h * P:(h + 1) * P], preferred_element_type=F32)
        a2 = jnp.dot(cq, w2_ref[:, h * 128:(h + 1) * 128], preferred_element_type=F32)
        o_ref[:, h * P:h * P + 128] = (a1[:, :128] * Q_PRESCALE).astype(BF16)
        o_ref[:, h * P + 128:(h + 1) * P] = ((a1[:, 128:] * cos + a2 * sin) * Q_PRESCALE).astype(BF16)


def mla_q(cq, w1, w2, cos_t, sin_t):
    M = cq.shape[0]
    tm = _row_tile(M, 640)
    row = lambda n: pl.BlockSpec((tm, n), lambda i: (i, 0))
    full = lambda a, b: pl.BlockSpec((a, b), lambda i: (0, 0))
    return pl.pallas_call(
        _mla_q_kernel,
        grid=(M // tm,),
        in_specs=[row(A_Q_LORA), full(*w1.shape), full(*w2.shape), row(128), row(128)],
        out_specs=row(A_HEADS * A_QK_PAD),
        out_shape=jax.ShapeDtypeStruct((M, A_HEADS * A_QK_PAD), BF16),
        compiler_params=_cparams(("parallel",)),
        name="mla_q",
    )(cq, w1, w2, cos_t, sin_t)


def _mla_kv_kernel(c_ref, wk_ref, wvt_ref, kn_ref, vt_ref):
    c = c_ref[...]
    kn_ref[...] = jnp.dot(c, wk_ref[...], preferred_element_type=F32).astype(BF16)
    vt_ref[...] = lax.dot_general(wvt_ref[...], c, (((1,), (1,)), ((), ())),
                                  preferred_element_type=F32).astype(BF16)


def mla_kv(ckvb, w_uk_cols, w_uv_rows, n_rows, tm):
    C = A_KV_LORA
    N = w_uk_cols.shape[1]
    return pl.pallas_call(
        _mla_kv_kernel,
        grid=(n_rows // tm,),
        in_specs=[pl.BlockSpec((tm, C), lambda i: (i, 0)),
                  pl.BlockSpec((C, N), lambda i: (0, 0)),
                  pl.BlockSpec((N, C), lambda i: (0, 0))],
        out_specs=[pl.BlockSpec((tm, N), lambda i: (i, 0)),
                   pl.BlockSpec((N, tm), lambda i: (0, i))],
        out_shape=[jax.ShapeDtypeStruct((n_rows, N), BF16),
                   jax.ShapeDtypeStruct((N, n_rows), BF16)],
        compiler_params=_cparams(("parallel",)),
        name="mla_kv",
    )(ckvb, w_uk_cols, w_uv_rows)


def _flash_kernel(q_ref, kn_ref, kr_ref, vt_ref, o_ref, m_ref, l_ref, acc_ref, *, half, n_pairs):
    nt_dims = (((1,), (1,)), ((), ()))

    def update(j, q0, k0, klen, masked):
        q = q_ref[pl.ds(q0, half), :]
        k = jnp.concatenate([kn_ref[pl.ds(k0, klen), :], kr_ref[pl.ds(k0, klen), :]], axis=1)
        st = lax.dot_general(k, q, nt_dims, preferred_element_type=F32)
        if masked:
            kpos = lax.broadcasted_iota(jnp.int32, st.shape, 0)
            qpos = lax.broadcasted_iota(jnp.int32, st.shape, 1)
            st = jnp.where(kpos <= qpos, st, -jnp.inf)
        m_prev = m_ref[j]
        m_new = jnp.maximum(m_prev, jnp.max(st, axis=0, keepdims=True))
        alpha = jnp.exp2(m_prev - m_new)
        p = jnp.exp2(st - m_new)
        l_ref[j] = alpha * l_ref[j] + jnp.sum(p, axis=0, keepdims=True)
        acc_ref[j] = alpha * acc_ref[j] + jnp.dot(vt_ref[:, pl.ds(k0, klen)], p.astype(BF16),
                                                  preferred_element_type=F32)
        m_ref[j] = m_new

    def pair(p, carry):
        qa = pl.multiple_of(p * (2 * half), 2 * half)
        qb = pl.multiple_of(qa + half, half)
        m_ref[...] = jnp.full(m_ref.shape, -jnp.inf, F32)
        l_ref[...] = jnp.zeros(l_ref.shape, F32)
        acc_ref[...] = jnp.zeros(acc_ref.shape, F32)

        def k_step(kc, c):
            k0 = pl.multiple_of(kc * (2 * half), 2 * half)
            update(0, qa, k0, 2 * half, False)
            update(1, qb, k0, 2 * half, False)
            return c

        lax.fori_loop(0, p, k_step, 0)
        update(0, qa, qa, half, True)
        update(1, qb, qa, half, False)
        update(1, qb, qb, half, True)
        o_ref[pl.ds(qa, half), :] = (acc_ref[0] / l_ref[0]).T.astype(o_ref.dtype)
        o_ref[pl.ds(qb, half), :] = (acc_ref[1] / l_ref[1]).T.astype(o_ref.dtype)
        return carry

    lax.fori_loop(0, n_pairs, pair, 0)


def mla_flash(qp, kn, krb, vt, n_batch, seq, half):
    H = A_HEADS
    assert seq % (2 * half) == 0
    return pl.pallas_call(
        functools.partial(_flash_kernel, half=half, n_pairs=seq // (2 * half)),
        grid=(n_batch, H),
        in_specs=[pl.BlockSpec((seq, A_QK_PAD), lambda b, h: (b, h)),
                  pl.BlockSpec((seq, A_NOPE), lambda b, h: (b, h)),
                  pl.BlockSpec((seq, 128), lambda b, h: (b, 0)),
                  pl.BlockSpec((A_VDIM, seq), lambda b, h: (h, b))],
        out_specs=pl.BlockSpec((seq, A_VDIM), lambda b, h: (b, h)),
        out_shape=jax.ShapeDtypeStruct((n_batch * seq, H * A_VDIM), BF16),
        scratch_shapes=[pltpu.VMEM((2, 1, half), F32), pltpu.VMEM((2, 1, half), F32),
                        pltpu.VMEM((2, A_VDIM, half), F32)],
        compiler_params=_cparams(("parallel", "parallel")),
        name="mla_flash",
    )(qp, kn, krb, vt)


def _decode_kernel(pt_ref, ql_ref, qr_ref, cn_ref, rn_ref, ckv_hbm, krt_hbm, o_ref,
                   cbuf, rbuf, sems, m_ref, l_ref, acc_ref,
                   *, layer, n_pages, chunk, group):
    g = pl.program_id(0)
    n_chunks = n_pages // chunk
    last = 2 * pl.num_programs(0) - 1
    c0 = (2 * g) % n_chunks

    def copies(k, slot):
        out = []
        rg = k // n_chunks
        cc = k % n_chunks
        for r in range(group):
            base = (rg * group + r) * n_pages + cc * chunk
            for p in range(chunk):
                page = pt_ref[base + p]
                out.append(pltpu.make_async_copy(ckv_hbm.at[layer, page], cbuf.at[slot, r, p],
                                                 sems.at[0, slot]))
                out.append(pltpu.make_async_copy(krt_hbm.at[layer, page], rbuf.at[slot, r, p],
                                                 sems.at[1, slot]))
        return out

    def start(k, slot):
        for cp in copies(k, slot):
            cp.start()

    def wait(k, slot):
        for cp in copies(k, slot):
            cp.wait()

    @pl.when(g == 0)
    def _():
        start(0, 0)

    @pl.when(c0 == 0)
    def _():
        for r in range(group):
            cn = cn_ref[r].astype(F32)
            rn = rn_ref[r].astype(F32)
            s_new = (jnp.sum(ql_ref[r].astype(F32) * cn, axis=-1, keepdims=True)
                     + jnp.sum(qr_ref[r].astype(F32) * rn, axis=-1, keepdims=True))
            m_ref[r] = s_new
            l_ref[r] = jnp.ones(s_new.shape, F32)
            acc_ref[r] = jnp.broadcast_to(cn, acc_ref.shape[1:])

    nt_dims = (((1,), (1,)), ((), ()))

    def attend(slot):
        cbs, scores = [], []
        for r in range(group):
            cb = cbuf[slot, r].reshape(chunk * PAGE_SIZE, A_KV_LORA).astype(BF16)
            s_rope = jnp.concatenate(
                [jnp.dot(qr_ref[r], rbuf[slot, r, p].astype(BF16), preferred_element_type=F32)
                 for p in range(chunk)], axis=1)
            cbs.append(cb)
            scores.append(lax.dot_general(ql_ref[r], cb, nt_dims, preferred_element_type=F32) + s_rope)
        for r in range(group):
            s = scores[r]
            m_prev = m_ref[r]
            m_new = jnp.maximum(m_prev, jnp.max(s, axis=-1, keepdims=True))
            alpha = jnp.exp2(m_prev - m_new)
            p = jnp.exp2(s - m_new)
            l_ref[r] = alpha * l_ref[r] + jnp.sum(p, axis=-1, keepdims=True)
            acc_ref[r] = alpha * acc_ref[r] + jnp.dot(p.astype(BF16), cbs[r], preferred_element_type=F32)
            m_ref[r] = m_new

    wait(2 * g, 0)
    start(2 * g + 1, 1)
    attend(0)
    wait(2 * g + 1, 1)
    start(jnp.minimum(2 * g + 2, last), 0)
    attend(1)

    @pl.when(c0 + 1 == n_chunks - 1)
    def _():
        for r in range(group):
            o_ref[r] = (acc_ref[r] / l_ref[r]).astype(o_ref.dtype)

    @pl.when(2 * g + 1 == last)
    def _():
        wait(last, 0)


def mla_decode(page_table, ql, qr, cn, rn, cache_ckv, cache_krt, layer, chunk=8, group=2):
    Bd, n_pages = page_table.shape
    assert n_pages % (2 * chunk) == 0 and Bd % group == 0
    steps_per_group = n_pages // (2 * chunk)
    H, C, R = A_HEADS, A_KV_LORA, A_ROPE
    req = lambda shp: pl.BlockSpec((group,) + shp, lambda g, pt: (g // steps_per_group, 0, 0))
    grid_spec = pltpu.PrefetchScalarGridSpec(
        num_scalar_prefetch=1,
        grid=(Bd // group * steps_per_group,),
        in_specs=[req((H, C)), req((H, R)), req((1, C)), req((1, R)),
                  pl.BlockSpec(memory_space=pl.ANY), pl.BlockSpec(memory_space=pl.ANY)],
        out_specs=req((H, C)),
        scratch_shapes=[pltpu.VMEM((2, group, chunk, PAGE_SIZE, C), F32),
                        pltpu.VMEM((2, group, chunk, R, PAGE_SIZE), F32),
                        pltpu.SemaphoreType.DMA((2, 2)),
                        pltpu.VMEM((group, H, 1), F32), pltpu.VMEM((group, H, 1), F32),
                        pltpu.VMEM((group, H, C), F32)])
    return pl.pallas_call(
        functools.partial(_decode_kernel, layer=layer, n_pages=n_pages, chunk=chunk, group=group),
        grid_spec=grid_spec,
        out_shape=jax.ShapeDtypeStruct((Bd, H, C), BF16),
        compiler_params=_cparams(("arbitrary",)),
        name="mla_decode",
    )(page_table.reshape(-1), ql, qr, cn, rn, cache_ckv, cache_krt)


def _rope_tables(pos):
    half = A_ROPE // 2
    inv = ROPE_THETA ** (-jnp.arange(half, dtype=F32) / half)
    ang = pos.astype(F32)[:, None] * inv[None, :]
    cos, sin = jnp.cos(ang), jnp.sin(ang)
    z = jnp.zeros((pos.shape[0], 128 - A_ROPE), F32)
    return (jnp.concatenate([cos, cos, z], axis=1), jnp.concatenate([sin, sin, z], axis=1))


def _half_swap(w):
    half = A_ROPE // 2
    return jnp.concatenate([-w[..., half:], w[..., :half]], axis=-1)


def mla_layer(x, g_mix, w_in, g_q, g_kv, w_uq, w_ukv, w_o_all, cache_ckv, cache_krt, page_table,
              layer, n_batch, seq, cos_t, sin_t, flash_tile):
    D = x.shape[1]
    Mp = n_batch * seq
    H = A_HEADS
    lo = A_Q_LORA + A_KV_LORA
    zpad = jnp.zeros((D, 128 - A_ROPE), F32)
    w_kr = w_in[:, lo:]
    w_in_ext = jnp.concatenate([w_in[:, :lo], w_kr, zpad, _half_swap(w_kr), zpad], axis=1).astype(BF16)
    cq, ckv, ckvb, kr, krb = mla_in(x, g_mix, w_in_ext, g_q, g_kv, cos_t, sin_t)

    wq = w_uq.reshape(A_Q_LORA, H, A_NOPE + A_ROPE)
    wq_rope = wq[..., A_NOPE:]
    zq = jnp.zeros((A_Q_LORA, H, 128 - A_ROPE), F32)
    w1 = jnp.concatenate([wq[..., :A_NOPE], wq_rope, zq], axis=-1).reshape(A_Q_LORA, H * A_QK_PAD)
    w2 = jnp.concatenate([_half_swap(wq_rope), zq], axis=-1).reshape(A_Q_LORA, H * 128)
    qp = mla_q(cq, w1.astype(BF16), w2.astype(BF16), cos_t, sin_t)

    wkv = w_ukv.reshape(A_KV_LORA, H, A_NOPE + A_VDIM)
    w_uk = wkv[..., :A_NOPE]
    w_uv = wkv[..., A_NOPE:]
    kn, vt = mla_kv(ckvb, w_uk.reshape(A_KV_LORA, H * A_NOPE).astype(BF16),
                    jnp.transpose(w_uv, (1, 2, 0)).reshape(H * A_VDIM, A_KV_LORA).astype(BF16),
                    Mp, min(1024, seq))
    o_p = mla_flash(qp, kn, krb, vt, n_batch, seq, flash_tile)

    Bd = page_table.shape[0]
    qs = qp[Mp:].reshape(Bd, H, A_QK_PAD)
    q_nope = qs[:, :, :A_NOPE].reshape(Bd, H * A_NOPE)
    q_rope = qs[:, :, A_NOPE:A_NOPE + A_ROPE]
    q_lat = head_mm(q_nope, jnp.transpose(w_uk, (1, 2, 0)).astype(BF16), BF16, "mla_qlat")
    o_lat = mla_decode(page_table, q_lat.reshape(Bd, H, A_KV_LORA), q_rope,
                       ckvb[Mp:].reshape(Bd, 1, A_KV_LORA),
                       krb[Mp:, :A_ROPE].reshape(Bd, 1, A_ROPE),
                       cache_ckv, cache_krt, layer)
    o_s = head_mm(o_lat.reshape(Bd, H * A_KV_LORA), jnp.transpose(w_uv, (1, 0, 2)).astype(BF16), BF16,
                  "mla_ouv")

    o = jnp.concatenate([o_p, o_s], axis=0)
    x = mm(o, w_o_all, F32, "mla_out", res=x, scale=1.0, layer=layer, tn=512)
    return x, ckv, kr[:, :A_ROPE]


def _lower_bound(raw, layer):
    e = jnp.exp(raw - jnp.max(raw, axis=0, keepdims=True))
    tot = jnp.sum(e, axis=0, keepdims=True)
    if layer == 0:
        return jnp.zeros_like(tot)
    return jnp.sum(e[1:layer + 1], axis=0, keepdims=True) / tot


def _hgrn_scan_kernel(q_ref, f_ref, i_ref, og_ref, lbr_ref, go_ref,
                      seg_ref, pm_ref, rm_ref, o_ref, st_ref, state_ref, a_ref, *, layer, tb, hb):
    t = pl.program_id(2)
    n_lev = pm_ref.shape[0]
    tn_dims = (((0,), (0,)), ((), ()))
    nt_dims = (((1,), (1,)), ((), ()))

    @pl.when(t == 0)
    def _():
        state_ref[...] = jnp.zeros(state_ref.shape, F32)

    lb = _lower_bound(lbr_ref[...], layer)
    gate = lb + (1.0 - lb) * _sigmoid(f_ref[...])
    logf = jnp.log(gate)
    q = _silu(q_ref[...])
    k = 1.0 - gate
    vb = i_ref[...].astype(BF16)
    a1 = logf.astype(BF16)
    r1 = logf - a1.astype(F32)
    a2 = r1.astype(BF16)
    a3 = (r1 - a2.astype(F32)).astype(BF16)

    def seg_sum(lev):
        m = seg_ref[lev]
        return (jnp.dot(m, a1, preferred_element_type=F32) + jnp.dot(m, a2, preferred_element_type=F32)
                + jnp.dot(m, a3, preferred_element_type=F32))

    for lev in range(n_lev):
        e = jnp.exp(seg_sum(lev))
        right = jnp.concatenate([rm_ref[lev]] * hb, axis=1)
        qh = (q * e * right).astype(BF16)
        kh = (k * e * (1.0 - right)).astype(BF16)
        for hh in range(hb):
            ls = slice(hh * B_DK, (hh + 1) * B_DK)
            s = lax.dot_general(qh[:, ls], kh[:, ls], nt_dims, preferred_element_type=F32) * pm_ref[lev]
            if lev == 0:
                a_ref[hh] = s
            else:
                a_ref[hh] = a_ref[hh] + s

    b = seg_sum(n_lev)
    b_last = b[tb - 1:tb, :]
    qe = (q * jnp.exp(b)).astype(BF16)
    kd = (k * jnp.exp(b_last - b)).astype(BF16)
    decay = jnp.exp(b_last)
    qk = q * k
    for hh in range(hb):
        ls = slice(hh * B_DK, (hh + 1) * B_DK)
        st = state_ref[hh]
        v = vb[:, ls]
        o = (lax.dot_general(qe[:, ls], st.astype(BF16), nt_dims, preferred_element_type=F32)
             + jnp.dot(a_ref[hh].astype(BF16), v, preferred_element_type=F32)
             + jnp.sum(qk[:, ls], axis=-1, keepdims=True) * i_ref[:, ls])
        state_ref[hh] = st * decay[:, ls] + lax.dot_general(v, kd[:, ls], tn_dims,
                                                            preferred_element_type=F32)
        o = _rms(o, go_ref[:, ls]) * _silu(og_ref[:, ls])
        o_ref[:, ls] = o.astype(o_ref.dtype)

    @pl.when(t == pl.num_programs(2) - 1)
    def _():
        for hh in range(hb):
            st_ref[0, hh] = state_ref[hh].T


def _scan_constants(tb):
    assert tb & (tb - 1) == 0
    t = jnp.arange(tb)[:, None]
    j = jnp.arange(tb)[None, :]
    segs, pms, rms = [], [], []
    h = tb // 2
    while h >= 1:
        split = (t // (2 * h)) * (2 * h) + h - 1
        right = t > split
        segs.append(jnp.where(right, (j > split) & (j <= t), (j > t) & (j <= split)))
        pms.append((t // (2 * h)) == (j // (2 * h)))
        rms.append(jnp.broadcast_to(right, (tb, B_DK)))
        h //= 2
    segs.append(j <= t)
    return (jnp.stack(segs).astype(BF16), jnp.stack(pms).astype(F32), jnp.stack(rms).astype(F32))


def hgrn_scan(proj, lb_raw, g_o, layer, n_batch, seq, tb, hb=4):
    H = B_HEADS
    assert H % hb == 0
    nt = seq // tb
    W = hb * B_DK
    seg, pm, rm = _scan_constants(tb)
    sec = lambda k: pl.BlockSpec((tb, W), lambda b, h, t, k=k: (b * nt + t, k * (H // hb) + h))
    const = lambda a: pl.BlockSpec(a.shape, lambda b, h, t: (0, 0, 0))
    return pl.pallas_call(
        functools.partial(_hgrn_scan_kernel, layer=layer, tb=tb, hb=hb),
        grid=(n_batch, H // hb, nt),
        in_specs=[sec(0), sec(1), sec(2), sec(3),
                  pl.BlockSpec((lb_raw.shape[0], W), lambda b, h, t: (0, h)),
                  pl.BlockSpec((1, W), lambda b, h, t: (0, h)),
                  const(seg), const(pm), const(rm)],
        out_specs=[pl.BlockSpec((tb, W), lambda b, h, t: (b * nt + t, h)),
                   pl.BlockSpec((1, hb, B_DK, B_DV), lambda b, h, t: (b, h, 0, 0))],
        out_shape=[jax.ShapeDtypeStruct((n_batch * seq, H * B_DV), BF16),
                   jax.ShapeDtypeStruct((n_batch, H, B_DK, B_DV), F32)],
        scratch_shapes=[pltpu.VMEM((hb, B_DV, B_DK), F32), pltpu.VMEM((hb, tb, tb), F32)],
        compiler_params=_cparams(("parallel", "parallel", "arbitrary")),
        name="hgrn_scan",
    )(proj, proj, proj, proj, lb_raw, g_o.reshape(1, -1), seg, pm, rm)


def _hgrn_step_kernel(p_ref, s_ref, lbr_ref, go_ref, o_ref, so_ref, *, layer):
    H = B_HEADS
    p = p_ref[0]
    lb = _lower_bound(lbr_ref[...], layer)[0]
    gate = lb + (1.0 - lb) * _sigmoid(p[H:2 * H])
    q = _silu(p[0:H])
    v = p[2 * H:3 * H]
    og = p[3 * H:4 * H]
    packed = jnp.concatenate([gate, q, jnp.zeros((128 - 2 * H, B_DK), F32)], axis=0)
    cols = packed.T
    outs = []
    for h in range(H):
        g_col = cols[:, h:h + 1]
        q_col = cols[:, H + h:H + h + 1]
        s_new = g_col * s_ref[0, h] + (1.0 - g_col) * v[h:h + 1, :]
        so_ref[0, h] = s_new
        outs.append(jnp.sum(q_col * s_new, axis=0, keepdims=True))
    o = jnp.concatenate(outs, axis=0)
    o_ref[0] = (_rms(o, go_ref[...]) * _silu(og)).astype(o_ref.dtype)


def hgrn_step(proj_s, state, lb_raw, g_o, layer):
    Bd = proj_s.shape[0]
    H = B_HEADS
    depth = lb_raw.shape[0]
    o, s_new = pl.pallas_call(
        functools.partial(_hgrn_step_kernel, layer=layer),
        grid=(Bd,),
        in_specs=[pl.BlockSpec((1, 4 * H, 128), lambda r: (r, 0, 0)),
                  pl.BlockSpec((1, H, B_DK, B_DV), lambda r: (r, 0, 0, 0)),
                  pl.BlockSpec((depth, H, B_DK), lambda r: (0, 0, 0)),
                  pl.BlockSpec((H, B_DV), lambda r: (0, 0))],
        out_specs=[pl.BlockSpec((1, H, B_DV), lambda r: (r, 0, 0)),
                   pl.BlockSpec((1, H, B_DK, B_DV), lambda r: (r, 0, 0, 0))],
        out_shape=[jax.ShapeDtypeStruct((Bd, H, B_DV), BF16),
                   jax.ShapeDtypeStruct(state.shape, F32)],
        compiler_params=_cparams(("parallel",)),
        name="hgrn_step",
    )(proj_s.reshape(Bd, 4 * H, 128), state, lb_raw.reshape(depth, H, B_DK), g_o.reshape(H, B_DV))
    return o.reshape(Bd, H * B_DV), s_new


def hgrn_layer(x, g_mix, w_in_all, lb_raw, g_o, w_o_all, state, j, layer, n_batch, seq, scan_tb):
    Mp = n_batch * seq
    proj = norm_mm(x, g_mix, w_in_all, (0,), w_in_all.shape[-1], 512, lambda a: a, F32, "hgrn_in", layer=j)
    o_p, st_p = hgrn_scan(proj, lb_raw, g_o, layer, n_batch, seq, scan_tb)
    o_s, st_s = hgrn_step(proj[Mp:], state, lb_raw, g_o, layer)
    o = jnp.concatenate([o_p, o_s], axis=0)
    x = mm(o, w_o_all, F32, "hgrn_out", res=x, scale=1.0, layer=j, tn=512)
    return x, st_p, st_s


def _gmlp_mix_kernel(u_ref, v_ref, gv_ref, ws_ref, bias_ref, vo_ref, z_ref):
    v = _rms(v_ref[...], gv_ref[...])
    vo_ref[...] = v
    vb = v.astype(BF16)
    L = v.shape[0]
    r_id = lax.broadcasted_iota(jnp.int32, (L, L), 0)
    c_id = lax.broadcasted_iota(jnp.int32, (L, L), 1)
    for g in range(C_GROUPS):
        sl = slice(g * C_GDIM, (g + 1) * C_GDIM)
        w = jnp.where(r_id >= c_id, ws_ref[g], 0.0).astype(BF16)
        mixed = jnp.dot(w, vb[:, sl], preferred_element_type=F32) + bias_ref[:, sl]
        z_ref[:, sl] = (u_ref[:, sl] * mixed).astype(z_ref.dtype)


def gmlp_mix(uv, g_v, w_s, bias_full, n_rows):
    W = g_v.shape[0]
    L = C_CHUNK
    return pl.pallas_call(
        _gmlp_mix_kernel,
        grid=(n_rows // L,),
        in_specs=[pl.BlockSpec((L, W), lambda c: (c, 0)),
                  pl.BlockSpec((L, W), lambda c: (c, 1)),
                  pl.BlockSpec((1, W), lambda c: (0, 0)),
                  pl.BlockSpec((C_GROUPS, L, L), lambda c: (0, 0, 0)),
                  pl.BlockSpec((L, W), lambda c: (0, 0))],
        out_specs=[pl.BlockSpec((L, W), lambda c: (c, 0)),
                   pl.BlockSpec((L, W), lambda c: (c, 0))],
        out_shape=[jax.ShapeDtypeStruct((n_rows, W), F32),
                   jax.ShapeDtypeStruct((n_rows, W), BF16)],
        compiler_params=_cparams(("parallel",)),
        name="gmlp_mix",
    )(uv, uv, g_v.reshape(1, W), w_s, bias_full)


def _gmlp_single_kernel(u_ref, v_ref, gv_ref, w0_ref, b0_ref, vo_ref, z_ref):
    v = _rms(v_ref[...], gv_ref[...])
    vo_ref[...] = v
    z_ref[...] = (u_ref[...] * (w0_ref[...] * v + b0_ref[...])).astype(z_ref.dtype)


def gmlp_single(uv_s, g_v, w0, b0):
    R = uv_s.shape[0]
    W = g_v.shape[0]
    blk = lambda j: pl.BlockSpec((R, W), lambda i, j=j: (0, j))
    vec = pl.BlockSpec((1, W), lambda i: (0, 0))
    return pl.pallas_call(
        _gmlp_single_kernel,
        grid=(1,),
        in_specs=[blk(0), blk(1), vec, vec, vec],
        out_specs=[blk(0), blk(0)],
        out_shape=[jax.ShapeDtypeStruct((R, W), F32), jax.ShapeDtypeStruct((R, W), BF16)],
        compiler_params=_cparams(("arbitrary",)),
        name="gmlp_single",
    )(uv_s, uv_s, g_v.reshape(1, W), w0, b0)


def gmlp_layer(x, g_mix, w_in_all, g_v, w_s, b_s, w_o_all, j, n_batch, seq):
    Mp = n_batch * seq
    W = g_v.shape[0]
    uv = norm_mm(x, g_mix, w_in_all, (0,), w_in_all.shape[-1], 512, _gelu_tanh, F32, "gmlp_in", layer=j)
    assert seq % C_CHUNK == 0
    bias_full = jnp.repeat(b_s[:, :C_CHUNK].T, C_GDIM, axis=1)
    v_p, z_p = gmlp_mix(uv, g_v, w_s[:, :C_CHUNK, :C_CHUNK], bias_full, Mp)
    w0 = jnp.repeat(w_s[:, 0, 0], C_GDIM).reshape(1, W)
    b0 = jnp.repeat(b_s[:, 0], C_GDIM).reshape(1, W)
    v_s, z_s = gmlp_single(uv[Mp:], g_v, w0, b0)
    z = jnp.concatenate([z_p, z_s], axis=0)
    x = mm(z, w_o_all, F32, "gmlp_out", res=x, scale=1.0, layer=j, tn=512)
    last = ((seq - 1) // C_CHUNK) * C_CHUNK
    v_last = v_p.reshape(n_batch, seq, W)[:, last:]
    return x, v_last, v_s


def _forward(x_prompt, x_sample, cache_a_ckv, cache_a_kr, state_b, page_table, norm_ffn_a,
             ffn_a_wi, ffn_a_wo, norm_mix, a_w_in, a_g_q, a_g_kv, a_w_uq, a_w_ukv, a_w_o,
             b_w_in, b_lower_bounds, b_g_o, b_w_o, c_w_in, c_g_v, c_w_s, c_b_s, c_w_o,
             norm_ffn_b, ffn_b_wi, ffn_b_wo, final_norm, *, flash_tile, scan_tb):
    n_batch, seq, D = x_prompt.shape
    Bd, dec_seq, _ = x_sample.shape
    assert dec_seq == 1
    depth = norm_mix.shape[0]
    Mp = n_batch * seq
    past_len = page_table.shape[1] * PAGE_SIZE
    x = jnp.concatenate([x_prompt.reshape(Mp, D), x_sample.reshape(Bd, D)], axis=0)

    pos = jnp.concatenate([jnp.tile(jnp.arange(seq), n_batch),
                           jnp.full((Bd,), past_len, jnp.int32)])
    cos_t, sin_t = _rope_tables(pos)

    cache_krt = jnp.swapaxes(cache_a_kr, 2, 3)

    a_ckv, a_kr, b_sp, b_ss, c_vp, c_vs = [], [], [], [], [], []
    for i in range(depth):
        x = ffn_half_step(x, norm_ffn_a[i], ffn_a_wi, ffn_a_wo, i)
        j = i // N_MIXERS
        kind = i % N_MIXERS
        if kind == 0:
            x, ckv, kr = mla_layer(x, norm_mix[i], a_w_in[j], a_g_q[j], a_g_kv[j], a_w_uq[j],
                                   a_w_ukv[j], a_w_o, cache_a_ckv, cache_krt, page_table,
                                   j, n_batch, seq, cos_t, sin_t, flash_tile)
            a_ckv.append(ckv)
            a_kr.append(kr)
        elif kind == 1:
            x, sp, ss = hgrn_layer(x, norm_mix[i], b_w_in, b_lower_bounds, b_g_o[j], b_w_o,
                                   state_b[j], j, i, n_batch, seq, scan_tb)
            b_sp.append(sp)
            b_ss.append(ss)
        else:
            x, vp, vs = gmlp_layer(x, norm_mix[i], c_w_in, c_g_v[j], c_w_s[j], c_b_s[j],
                                   c_w_o, j, n_batch, seq)
            c_vp.append(vp)
            c_vs.append(vs)
        x = ffn_half_step(x, norm_ffn_b[i], ffn_b_wi, ffn_b_wo, i)

    y_p = final_norm_call(x, final_norm, 0, Mp, min(1024, Mp))
    y_s = final_norm_call(x, final_norm, Mp, Bd, Bd)
    ckv_all = jnp.stack(a_ckv)
    kr_all = jnp.stack(a_kr)
    return (y_p.reshape(n_batch, seq, D), y_s.reshape(Bd, 1, D),
            ckv_all[:, :Mp].reshape(-1, n_batch, seq, A_KV_LORA),
            kr_all[:, :Mp].reshape(-1, n_batch, seq, A_ROPE),
            ckv_all[:, Mp:].reshape(-1, Bd, 1, A_KV_LORA),
            kr_all[:, Mp:].reshape(-1, Bd, 1, A_ROPE),
            jnp.stack(b_sp), jnp.stack(b_ss),
            jnp.stack(c_vp), jnp.stack(c_vs)[:, :, None, :])


def kernel(x_prompt, x_sample, cache_a_ckv, cache_a_kr, state_b, page_table, norm_ffn_a, ffn_a_wi, ffn_a_wo, norm_mix, a_w_in, a_g_q, a_g_kv, a_w_uq, a_w_ukv, a_w_o, b_w_in, b_lower_bounds, b_g_o, b_w_o, c_w_in, c_g_v, c_w_s, c_b_s, c_w_o, norm_ffn_b, ffn_b_wi, ffn_b_wo, final_norm):
    seq = x_prompt.shape[1]
    return _forward(x_prompt, x_sample, cache_a_ckv, cache_a_kr, state_b, page_table, norm_ffn_a,
                    ffn_a_wi, ffn_a_wo, norm_mix, a_w_in, a_g_q, a_g_kv, a_w_uq, a_w_ukv, a_w_o,
                    b_w_in, b_lower_bounds, b_g_o, b_w_o, c_w_in, c_g_v, c_w_s, c_b_s, c_w_o,
                    norm_ffn_b, ffn_b_wi, ffn_b_wo, final_norm,
                    flash_tile=min(512, seq // 2), scan_tb=min(256, seq))
```

```python
import functools
import math

import jax
import jax.numpy as jnp
from jax import lax
from jax.experimental import pallas as pl
from jax.experimental.pallas import tpu as pltpu

F32 = jnp.float32
BF16 = jnp.bfloat16

EPS = 1e-6
ROPE_THETA = 10000.0
N_MIXERS = 3

A_HEADS = 16
A_NOPE = 128
A_ROPE = 64
A_VDIM = 128
A_Q_LORA = 512
A_KV_LORA = 512
A_QK_PAD = 256
Q_PRESCALE = (A_NOPE + A_ROPE) ** -0.5 * math.log2(math.e)
PAGE_SIZE = 128

B_HEADS = 16
B_DK = 128
B_DV = 128

C_GROUPS = 16
C_GDIM = 128
C_CHUNK = 128

MIB = 1024 * 1024


def _cparams(semantics, vmem_mib=48):
    return pltpu.CompilerParams(dimension_semantics=semantics,
                                vmem_limit_bytes=vmem_mib * MIB)


def _rms(x, g):
    return x * lax.rsqrt(jnp.mean(x * x, axis=-1, keepdims=True) + EPS) * g


def _sigmoid(x):
    return 1.0 / (1.0 + jnp.exp(-x))


def _silu(x):
    return x * _sigmoid(x)


def _gelu_tanh(x):
    c = math.sqrt(2.0 / math.pi)
    return 0.5 * x * (1.0 + jnp.tanh(c * (x + 0.044715 * (x * x * x))))


def _row_tile(m, cap):
    best = None
    for t in range(16, cap + 1, 16):
        if m % t == 0:
            best = t
    assert best is not None, (m, cap)
    return best


def _norm_mm_kernel(x_ref, g_ref, *refs, n_w, epilogue):
    w_refs = refs[:n_w]
    o_ref = refs[n_w]
    h_ref = refs[n_w + 1]

    @pl.when(pl.program_id(1) == 0)
    def _():
        h_ref[...] = _rms(x_ref[...], g_ref[...]).astype(BF16)

    h = h_ref[...]
    accs = [jnp.dot(h, w[...].astype(BF16), preferred_element_type=F32) for w in w_refs]
    o_ref[...] = epilogue(*accs).astype(o_ref.dtype)


def _weight_spec(w, layer, rows, tn, col_block):
    if layer is None:
        assert w.ndim == 2
        return pl.BlockSpec((rows, tn), lambda i, j: (0, col_block(i, j)))
    assert w.ndim == 3
    return pl.BlockSpec((None, rows, tn), lambda i, j: (layer, 0, col_block(i, j)))


def norm_mm(x, g, w, col_offsets, out_cols, tn, epilogue, out_dtype, name, layer=None, tm_cap=1040):
    M, D = x.shape
    tm = _row_tile(M, tm_cap)
    n_w = len(col_offsets)
    in_specs = [pl.BlockSpec((tm, D), lambda i, j: (i, 0)),
                pl.BlockSpec((1, D), lambda i, j: (0, 0))]
    for off in col_offsets:
        assert off % tn == 0
        in_specs.append(_weight_spec(w, layer, D, tn, lambda i, j, o=off // tn: j + o))
    return pl.pallas_call(
        functools.partial(_norm_mm_kernel, n_w=n_w, epilogue=epilogue),
        grid=(M // tm, out_cols // tn),
        in_specs=in_specs,
        out_specs=pl.BlockSpec((tm, tn), lambda i, j: (i, j)),
        out_shape=jax.ShapeDtypeStruct((M, out_cols), out_dtype),
        scratch_shapes=[pltpu.VMEM((tm, D), BF16)],
        compiler_params=_cparams(("parallel", "arbitrary")),
        name=name,
    )(x, g.reshape(1, D), *([w] * n_w))


def _mm_kernel(a_ref, w_ref, *refs, scale, has_res):
    o_ref = refs[-1]
    acc = jnp.dot(a_ref[...], w_ref[...].astype(BF16), preferred_element_type=F32)
    if has_res:
        acc = refs[0][...] + scale * acc
    o_ref[...] = acc.astype(o_ref.dtype)


def mm(a, w, out_dtype, name, res=None, scale=1.0, layer=None, tm_cap=1040, tn=512, vmem_mib=48):
    M, K = a.shape
    N = w.shape[-1]
    tm = _row_tile(M, tm_cap)
    tn = min(tn, N)
    in_specs = [pl.BlockSpec((tm, K), lambda i, j: (i, 0)),
                _weight_spec(w, layer, K, tn, lambda i, j: j)]
    args = [a, w]
    if res is not None:
        in_specs.append(pl.BlockSpec((tm, tn), lambda i, j: (i, j)))
        args.append(res)
    return pl.pallas_call(
        functools.partial(_mm_kernel, scale=scale, has_res=res is not None),
        grid=(M // tm, N // tn),
        in_specs=in_specs,
        out_specs=pl.BlockSpec((tm, tn), lambda i, j: (i, j)),
        out_shape=jax.ShapeDtypeStruct((M, N), out_dtype),
        compiler_params=_cparams(("parallel", "parallel"), vmem_mib),
        name=name,
    )(*args)


def _head_mm_kernel(x_ref, w_ref, o_ref, *, heads, k, n):
    for h in range(heads):
        o_ref[:, h * n:(h + 1) * n] = jnp.dot(
            x_ref[:, h * k:(h + 1) * k], w_ref[h], preferred_element_type=F32).astype(o_ref.dtype)


def head_mm(x, w, out_dtype, name):
    R = x.shape[0]
    H, K, N = w.shape
    return pl.pallas_call(
        functools.partial(_head_mm_kernel, heads=H, k=K, n=N),
        grid=(1,),
        in_specs=[pl.BlockSpec((R, H * K), lambda i: (0, 0)),
                  pl.BlockSpec((H, K, N), lambda i: (0, 0, 0))],
        out_specs=pl.BlockSpec((R, H * N), lambda i: (0, 0)),
        out_shape=jax.ShapeDtypeStruct((R, H * N), out_dtype),
        compiler_params=_cparams(("arbitrary",)),
        name=name,
    )(x, w)


def _norm_kernel(x_ref, g_ref, o_ref):
    o_ref[...] = _rms(x_ref[...], g_ref[...])


def final_norm_call(x, g, row0, n_rows, tm):
    D = x.shape[1]
    assert row0 % tm == 0 and n_rows % tm == 0
    blk0 = row0 // tm
    return pl.pallas_call(
        _norm_kernel,
        grid=(n_rows // tm,),
        in_specs=[pl.BlockSpec((tm, D), lambda i: (i + blk0, 0)),
                  pl.BlockSpec((1, D), lambda i: (0, 0))],
        out_specs=pl.BlockSpec((tm, D), lambda i: (i, 0)),
        out_shape=jax.ShapeDtypeStruct((n_rows, D), F32),
        compiler_params=_cparams(("parallel",)),
        name="final_norm",
    )(x, g.reshape(1, D))


def _swiglu_epilogue(gate, up):
    return _silu(gate) * up


def ffn_half_step(x, g, wi, wo, layer):
    d_ff = wo.shape[1]
    hid = norm_mm(x, g, wi, (0, d_ff), d_ff, 512, _swiglu_epilogue, BF16, "ffn_in", layer=layer)
    return mm(hid, wo, F32, "ffn_out", res=x, scale=0.5, layer=layer, tm_cap=832, tn=512, vmem_mib=56)


def _mla_in_kernel(x_ref, g_ref, w_ref, gq_ref, gkv_ref, cos_ref, sin_ref,
                   cq_ref, ckv_ref, ckvb_ref, kr_ref, krb_ref):
    h = _rms(x_ref[...], g_ref[...]).astype(BF16)
    acc = jnp.dot(h, w_ref[...], preferred_element_type=F32)
    ql, kl = A_Q_LORA, A_KV_LORA
    cq_ref[...] = _rms(acc[:, :ql], gq_ref[...]).astype(BF16)
    ckv = _rms(acc[:, ql:ql + kl], gkv_ref[...])
    ckv_ref[...] = ckv
    ckvb_ref[...] = ckv.astype(BF16)
    kr = (acc[:, ql + kl:ql + kl + 128] * cos_ref[...]
          + acc[:, ql + kl + 128:ql + kl + 256] * sin_ref[...])
    kr_ref[...] = kr
    krb_ref[...] = kr.astype(BF16)


def mla_in(x, g, w_ext, g_q, g_kv, cos_t, sin_t):
    M, D = x.shape
    tm = _row_tile(M, 640)
    NW = w_ext.shape[1]
    row = lambda n: pl.BlockSpec((tm, n), lambda i: (i, 0))
    full = lambda a, b: pl.BlockSpec((a, b), lambda i: (0, 0))
    return pl.pallas_call(
        _mla_in_kernel,
        grid=(M // tm,),
        in_specs=[row(D), full(1, D), full(D, NW), full(1, A_Q_LORA), full(1, A_KV_LORA),
                  row(128), row(128)],
        out_specs=[row(A_Q_LORA), row(A_KV_LORA), row(A_KV_LORA), row(128), row(128)],
        out_shape=[jax.ShapeDtypeStruct((M, A_Q_LORA), BF16),
                   jax.ShapeDtypeStruct((M, A_KV_LORA), F32),
                   jax.ShapeDtypeStruct((M, A_KV_LORA), BF16),
                   jax.ShapeDtypeStruct((M, 128), F32),
                   jax.ShapeDtypeStruct((M, 128), BF16)],
        compiler_params=_cparams(("parallel",)),
        name="mla_in",
    )(x, g.reshape(1, D), w_ext, g_q.reshape(1, -1), g_kv.reshape(1, -1), cos_t, sin_t)


def _mla_q_kernel(cq_ref, w1_ref, w2_ref, cos_ref, sin_ref, o_ref):
    cq = cq_ref[...]
    cos = cos_ref[...]
    sin = sin_ref[...]
    P = A_QK_PAD
    for h in range(A_HEADS):
        a1 = jnp.dot(cq, w1_ref[:, h * P:(h + 1) * P], preferred_element_type=F32)
        a2 = jnp.dot(cq, w2_ref[:, h * 128:(h + 1) * 128], preferred_element_type=F32)
        o_ref[:, h * P:h * P + 128] = (a1[:, :128] * Q_PRESCALE).astype(BF16)
        o_ref[:, h * P + 128:(h + 1) * P] = ((a1[:, 128:] * cos + a2 * sin) * Q_PRESCALE).astype(BF16)


def mla_q(cq, w1, w2, cos_t, sin_t):
    M = cq.shape[0]
    tm = _row_tile(M, 640)
    row = lambda n: pl.BlockSpec((tm, n), lambda i: (i, 0))
    full = lambda a, b: pl.BlockSpec((a, b), lambda i: (0, 0))
    return pl.pallas_call(
        _mla_q_kernel,
        grid=(M // tm,),
        in_specs=[row(A_Q_LORA), full(*w1.shape), full(*w2.shape), row(128), row(128)],
        out_specs=row(A_HEADS * A_QK_PAD),
        out_shape=jax.ShapeDtypeStruct((M, A_HEADS * A_QK_PAD), BF16),
        compiler_params=_cparams(("parallel",)),
        name="mla_q",
    )(cq, w1, w2, cos_t, sin_t)


def _mla_kv_kernel(c_ref, wk_ref, wvt_ref, kn_ref, vt_ref):
    c = c_ref[...]
    kn_ref[...] = jnp.dot(c, wk_ref[...], preferred_element_type=F32).astype(BF16)
    vt_ref[...] = lax.dot_general(wvt_ref[...], c, (((1,), (1,)), ((), ())),
                                  preferred_element_type=F32).astype(BF16)


def mla_kv(ckvb, w_uk_cols, w_uv_rows, n_rows, tm):
    C = A_KV_LORA
    N = w_uk_cols.shape[1]
    NV = w_uv_rows.shape[0]
    return pl.pallas_call(
        _mla_kv_kernel,
        grid=(n_rows // tm,),
        in_specs=[pl.BlockSpec((tm, C), lambda i: (i, 0)),
                  pl.BlockSpec((C, N), lambda i: (0, 0)),
                  pl.BlockSpec((N, C), lambda i: (0, 0))],
        out_specs=[pl.BlockSpec((tm, N), lambda i: (i, 0)),
                   pl.BlockSpec((NV, tm), lambda i: (0, i))],
        out_shape=[jax.ShapeDtypeStruct((n_rows, N), BF16),
                   jax.ShapeDtypeStruct((NV, n_rows), BF16)],
        compiler_params=_cparams(("parallel",)),
        name="mla_kv",
    )(ckvb, w_uk_cols, w_uv_rows)


def _flash_kernel(q_ref, kn_ref, kr_ref, vt_ref, o_ref, m_ref, l_ref, acc_ref, *, half, n_pairs):
    nt_dims = (((1,), (1,)), ((), ()))

    def update(j, q0, k0, klen, masked):
        q = q_ref[pl.ds(q0, half), :]
        k = jnp.concatenate([kn_ref[pl.ds(k0, klen), :], kr_ref[pl.ds(k0, klen), :]], axis=1)
        st = lax.dot_general(k, q, nt_dims, preferred_element_type=F32)
        if masked:
            kpos = lax.broadcasted_iota(jnp.int32, st.shape, 0)
            qpos = lax.broadcasted_iota(jnp.int32, st.shape, 1)
            st = jnp.where(kpos <= qpos, st, -jnp.inf)
        m_prev = m_ref[j]
        m_new = jnp.maximum(m_prev, jnp.max(st, axis=0, keepdims=True))
        alpha = jnp.exp2(m_prev - m_new)
        p = jnp.exp2(st - m_new)
        l_ref[j] = alpha * l_ref[j] + jnp.sum(p, axis=0, keepdims=True)
        acc_ref[j] = alpha * acc_ref[j] + jnp.dot(vt_ref[:, pl.ds(k0, klen)], p.astype(BF16),
                                                  preferred_element_type=F32)
        m_ref[j] = m_new

    def finish(j):
        return (acc_ref[j] / l_ref[j]).T.astype(o_ref.dtype)

    def pair(p, carry):
        qa = pl.multiple_of(p * (2 * half), 2 * half)
        qb = pl.multiple_of(qa + half, half)
        m_ref[...] = jnp.full(m_ref.shape, -jnp.inf, F32)
        l_ref[...] = jnp.zeros(l_ref.shape, F32)
        acc_ref[...] = jnp.zeros(acc_ref.shape, F32)

        def k_step(kc, c):
            k0 = pl.multiple_of(kc * (2 * half), 2 * half)
            update(0, qa, k0, 2 * half, False)
            update(1, qb, k0, 2 * half, False)
            return c

        lax.fori_loop(0, p, k_step, 0)
        update(0, qa, qa, half, True)
        update(1, qb, qa, half, False)
        update(1, qb, qb, half, True)
        o_ref[pl.ds(qa, half), :] = finish(0)
        o_ref[pl.ds(qb, half), :] = finish(1)
        return carry

    lax.fori_loop(0, n_pairs, pair, 0)


def mla_flash(qp, kn, krb, vt, n_batch, seq, half):
    H = A_HEADS
    assert seq % (2 * half) == 0
    return pl.pallas_call(
        functools.partial(_flash_kernel, half=half, n_pairs=seq // (2 * half)),
        grid=(n_batch, H),
        in_specs=[pl.BlockSpec((seq, A_QK_PAD), lambda b, h: (b, h)),
                  pl.BlockSpec((seq, A_NOPE), lambda b, h: (b, h)),
                  pl.BlockSpec((seq, 128), lambda b, h: (b, 0)),
                  pl.BlockSpec((A_VDIM, seq), lambda b, h: (h, b))],
        out_specs=pl.BlockSpec((seq, A_VDIM), lambda b, h: (b, h)),
        out_shape=jax.ShapeDtypeStruct((n_batch * seq, H * A_VDIM), BF16),
        scratch_shapes=[pltpu.VMEM((2, 1, half), F32), pltpu.VMEM((2, 1, half), F32),
                        pltpu.VMEM((2, A_VDIM, half), F32)],
        compiler_params=_cparams(("parallel", "parallel")),
        name="mla_flash",
    )(qp, kn, krb, vt)


def _decode_kernel(pt_ref, ql_ref, qr_ref, cn_ref, rn_ref, ckv_hbm, krt_hbm, o_ref,
                   cbuf, rbuf, sems, m_ref, l_ref, acc_ref,
                   *, layer, n_pages, chunk, group):
    g = pl.program_id(0)
    n_chunks = n_pages // chunk
    last = 2 * pl.num_programs(0) - 1
    c0 = (2 * g) % n_chunks

    def copies(k, slot):
        out = []
        rg = k // n_chunks
        cc = k % n_chunks
        for r in range(group):
            base = (rg * group + r) * n_pages + cc * chunk
            for p in range(chunk):
                page = pt_ref[base + p]
                out.append(pltpu.make_async_copy(ckv_hbm.at[layer, page], cbuf.at[slot, r, p],
                                                 sems.at[0, slot]))
                out.append(pltpu.make_async_copy(krt_hbm.at[layer, page], rbuf.at[slot, r, p],
                                                 sems.at[1, slot]))
        return out

    def start(k, slot):
        for cp in copies(k, slot):
            cp.start()

    def wait(k, slot):
        for cp in copies(k, slot):
            cp.wait()

    @pl.when(g == 0)
    def _():
        start(0, 0)

    @pl.when(c0 == 0)
    def _():
        for r in range(group):
            cn = cn_ref[r].astype(F32)
            rn = rn_ref[r].astype(F32)
            s_new = (jnp.sum(ql_ref[r].astype(F32) * cn, axis=-1, keepdims=True)
                     + jnp.sum(qr_ref[r].astype(F32) * rn, axis=-1, keepdims=True))
            m_ref[r] = s_new
            l_ref[r] = jnp.ones(s_new.shape, F32)
            acc_ref[r] = jnp.broadcast_to(cn, acc_ref.shape[1:])

    nt_dims = (((1,), (1,)), ((), ()))

    def attend(slot):
        cbs, scores = [], []
        for r in range(group):
            cb = cbuf[slot, r].reshape(chunk * PAGE_SIZE, A_KV_LORA).astype(BF16)
            s_rope = jnp.concatenate(
                [jnp.dot(qr_ref[r], rbuf[slot, r, p].astype(BF16), preferred_element_type=F32)
                 for p in range(chunk)], axis=1)
            cbs.append(cb)
            scores.append(lax.dot_general(ql_ref[r], cb, nt_dims, preferred_element_type=F32) + s_rope)
        for r in range(group):
            s = scores[r]
            m_prev = m_ref[r]
            m_new = jnp.maximum(m_prev, jnp.max(s, axis=-1, keepdims=True))
            alpha = jnp.exp2(m_prev - m_new)
            p = jnp.exp2(s - m_new)
            l_ref[r] = alpha * l_ref[r] + jnp.sum(p, axis=-1, keepdims=True)
            acc_ref[r] = alpha * acc_ref[r] + jnp.dot(p.astype(BF16), cbs[r], preferred_element_type=F32)
            m_ref[r] = m_new

    start(2 * g + 1, 1)
    wait(2 * g, 0)
    attend(0)
    start(jnp.minimum(2 * g + 2, last), 0)
    wait(2 * g + 1, 1)
    attend(1)

    @pl.when(c0 + 1 == n_chunks - 1)
    def _():
        for r in range(group):
            o_ref[r] = (acc_ref[r] / l_ref[r]).astype(o_ref.dtype)

    @pl.when(2 * g + 1 == last)
    def _():
        wait(last, 0)


def mla_decode(page_table, ql, qr, cn, rn, cache_ckv, cache_krt, layer, chunk=8, group=2):
    Bd, n_pages = page_table.shape
    assert n_pages % (2 * chunk) == 0 and Bd % group == 0
    steps_per_group = n_pages // (2 * chunk)
    H, C, R = A_HEADS, A_KV_LORA, A_ROPE
    req = lambda shp: pl.BlockSpec((group,) + shp, lambda g, pt: (g // steps_per_group, 0, 0))
    grid_spec = pltpu.PrefetchScalarGridSpec(
        num_scalar_prefetch=1,
        grid=(Bd // group * steps_per_group,),
        in_specs=[req((H, C)), req((H, R)), req((1, C)), req((1, R)),
                  pl.BlockSpec(memory_space=pl.ANY), pl.BlockSpec(memory_space=pl.ANY)],
        out_specs=req((H, C)),
        scratch_shapes=[pltpu.VMEM((2, group, chunk, PAGE_SIZE, C), F32),
                        pltpu.VMEM((2, group, chunk, R, PAGE_SIZE), F32),
                        pltpu.SemaphoreType.DMA((2, 2)),
                        pltpu.VMEM((group, H, 1), F32), pltpu.VMEM((group, H, 1), F32),
                        pltpu.VMEM((group, H, C), F32)])
    return pl.pallas_call(
        functools.partial(_decode_kernel, layer=layer, n_pages=n_pages, chunk=chunk, group=group),
        grid_spec=grid_spec,
        out_shape=jax.ShapeDtypeStruct((Bd, H, C), BF16),
        compiler_params=_cparams(("arbitrary",)),
        name="mla_decode",
    )(page_table.reshape(-1), ql, qr, cn, rn, cache_ckv, cache_krt)


def _rope_tables(pos):
    half = A_ROPE // 2
    inv = ROPE_THETA ** (-jnp.arange(half, dtype=F32) / half)
    ang = pos.astype(F32)[:, None] * inv[None, :]
    cos, sin = jnp.cos(ang), jnp.sin(ang)
    z = jnp.zeros((pos.shape[0], 128 - A_ROPE), F32)
    return (jnp.concatenate([cos, cos, z], axis=1), jnp.concatenate([sin, sin, z], axis=1))


def _half_swap(w):
    half = A_ROPE // 2
    return jnp.concatenate([-w[..., half:], w[..., :half]], axis=-1)


def mla_layer(x, g_mix, w_in, g_q, g_kv, w_uq, w_ukv, w_o_all, cache_ckv, cache_krt, page_table,
              layer, n_batch, seq, cos_t, sin_t, flash_tile):
    D = x.shape[1]
    Mp = n_batch * seq
    H = A_HEADS
    lo = A_Q_LORA + A_KV_LORA
    zpad = jnp.zeros((D, 128 - A_ROPE), F32)
    w_kr = w_in[:, lo:]
    w_in_ext = jnp.concatenate([w_in[:, :lo], w_kr, zpad, _half_swap(w_kr), zpad], axis=1).astype(BF16)
    cq, ckv, ckvb, kr, krb = mla_in(x, g_mix, w_in_ext, g_q, g_kv, cos_t, sin_t)

    wq = w_uq.reshape(A_Q_LORA, H, A_NOPE + A_ROPE)
    wq_rope = wq[..., A_NOPE:]
    zq = jnp.zeros((A_Q_LORA, H, 128 - A_ROPE), F32)
    w1 = jnp.concatenate([wq[..., :A_NOPE], wq_rope, zq], axis=-1).reshape(A_Q_LORA, H * A_QK_PAD)
    w2 = jnp.concatenate([_half_swap(wq_rope), zq], axis=-1).reshape(A_Q_LORA, H * 128)
    qp = mla_q(cq, w1.astype(BF16), w2.astype(BF16), cos_t, sin_t)

    wkv = w_ukv.reshape(A_KV_LORA, H, A_NOPE + A_VDIM)
    w_uk = wkv[..., :A_NOPE]
    w_uv = wkv[..., A_NOPE:]
    kn, vt = mla_kv(ckvb, w_uk.reshape(A_KV_LORA, H * A_NOPE).astype(BF16),
                    jnp.transpose(w_uv, (1, 2, 0)).reshape(H * A_VDIM, A_KV_LORA).astype(BF16),
                    Mp, min(1024, seq))
    o_p = mla_flash(qp, kn, krb, vt, n_batch, seq, flash_tile)

    Bd = page_table.shape[0]
    qs = qp[Mp:].reshape(Bd, H, A_QK_PAD)
    q_nope = qs[:, :, :A_NOPE].reshape(Bd, H * A_NOPE)
    q_rope = qs[:, :, A_NOPE:A_NOPE + A_ROPE]
    q_lat = head_mm(q_nope, jnp.transpose(w_uk, (1, 2, 0)).astype(BF16), BF16, "mla_qlat")
    o_lat = mla_decode(page_table, q_lat.reshape(Bd, H, A_KV_LORA), q_rope,
                       ckvb[Mp:].reshape(Bd, 1, A_KV_LORA),
                       krb[Mp:, :A_ROPE].reshape(Bd, 1, A_ROPE),
                       cache_ckv, cache_krt, layer)
    o_s = head_mm(o_lat.reshape(Bd, H * A_KV_LORA), jnp.transpose(w_uv, (1, 0, 2)).astype(BF16), BF16,
                  "mla_ouv")

    o = jnp.concatenate([o_p, o_s], axis=0)
    x = mm(o, w_o_all, F32, "mla_out", res=x, scale=1.0, layer=layer, tn=512)
    return x, ckv, kr[:, :A_ROPE]


def _lower_bound(raw, layer):
    e = jnp.exp(raw - jnp.max(raw, axis=0, keepdims=True))
    tot = jnp.sum(e, axis=0, keepdims=True)
    if layer == 0:
        return jnp.zeros_like(tot)
    return jnp.sum(e[1:layer + 1], axis=0, keepdims=True) / tot


def _hgrn_scan_kernel(q_ref, f_ref, i_ref, og_ref, lbr_ref, go_ref,
                      seg_ref, pm_ref, rm_ref, o_ref, st_ref, state_ref, a_ref, *, layer, tb, hb):
    t = pl.program_id(2)
    n_lev = pm_ref.shape[0]
    tn_dims = (((0,), (0,)), ((), ()))
    nt_dims = (((1,), (1,)), ((), ()))

    @pl.when(t == 0)
    def _():
        state_ref[...] = jnp.zeros(state_ref.shape, F32)

    lb = _lower_bound(lbr_ref[...], layer)
    gate = lb + (1.0 - lb) * _sigmoid(f_ref[...])
    logf = jnp.log(gate)
    q = _silu(q_ref[...])
    k = 1.0 - gate
    vb = i_ref[...].astype(BF16)
    a1 = logf.astype(BF16)
    r1 = logf - a1.astype(F32)
    a2 = r1.astype(BF16)
    a3 = (r1 - a2.astype(F32)).astype(BF16)

    def seg_sum(lev):
        m = seg_ref[lev]
        return (jnp.dot(m, a1, preferred_element_type=F32) + jnp.dot(m, a2, preferred_element_type=F32)
                + jnp.dot(m, a3, preferred_element_type=F32))

    b = seg_sum(n_lev)
    width = b.shape[1]

    for lev in range(n_lev):
        right = jnp.concatenate([rm_ref[lev]] * hb, axis=1)
        blk = tb >> lev
        if blk >= 8:
            b3 = b.reshape(tb // blk, blk, width)
            b_split = jnp.broadcast_to(b3[:, blk // 2 - 1:blk // 2, :], b3.shape).reshape(tb, width)
            x = jnp.where(right > 0.5, b - b_split, b_split - b)
        else:
            x = seg_sum(lev)
        e = jnp.exp(x)
        qh = (q * e * right).astype(BF16)
        kh = (k * e * (1.0 - right)).astype(BF16)
        for hh in range(hb):
            ls = slice(hh * B_DK, (hh + 1) * B_DK)
            s = lax.dot_general(qh[:, ls], kh[:, ls], nt_dims, preferred_element_type=F32) * pm_ref[lev]
            if lev == 0:
                a_ref[hh] = s
            else:
                a_ref[hh] = a_ref[hh] + s

    b_last = b[tb - 1:tb, :]
    qe = (q * jnp.exp(b)).astype(BF16)
    kd = (k * jnp.exp(b_last - b)).astype(BF16)
    decay = jnp.exp(b_last)
    qk = q * k
    for hh in range(hb):
        ls = slice(hh * B_DK, (hh + 1) * B_DK)
        st = state_ref[hh]
        v = vb[:, ls]
        o = (lax.dot_general(qe[:, ls], st.astype(BF16), nt_dims, preferred_element_type=F32)
             + jnp.dot(a_ref[hh].astype(BF16), v, preferred_element_type=F32)
             + jnp.sum(qk[:, ls], axis=-1, keepdims=True) * i_ref[:, ls])
        state_ref[hh] = st * decay[:, ls] + lax.dot_general(v, kd[:, ls], tn_dims,
                                                            preferred_element_type=F32)
        o = _rms(o, go_ref[:, ls]) * _silu(og_ref[:, ls])
        o_ref[:, ls] = o.astype(o_ref.dtype)

    @pl.when(t == pl.num_programs(2) - 1)
    def _():
        for hh in range(hb):
            st_ref[0, hh] = state_ref[hh].T


def _scan_constants(tb):
    assert tb & (tb - 1) == 0
    t = jnp.arange(tb)[:, None]
    j = jnp.arange(tb)[None, :]
    segs, pms, rms = [], [], []
    h = tb // 2
    while h >= 1:
        split = (t // (2 * h)) * (2 * h) + h - 1
        right = t > split
        segs.append(jnp.where(right, (j > split) & (j <= t), (j > t) & (j <= split)))
        pms.append((t // (2 * h)) == (j // (2 * h)))
        rms.append(jnp.broadcast_to(right, (tb, B_DK)))
        h //= 2
    segs.append(j <= t)
    return (jnp.stack(segs).astype(BF16), jnp.stack(pms).astype(F32), jnp.stack(rms).astype(F32))


def hgrn_scan(proj, lb_raw, g_o, layer, n_batch, seq, tb, hb=4):
    H = B_HEADS
    assert H % hb == 0
    nt = seq // tb
    W = hb * B_DK
    seg, pm, rm = _scan_constants(tb)
    sec = lambda k: pl.BlockSpec((tb, W), lambda b, h, t, k=k: (b * nt + t, k * (H // hb) + h))
    const = lambda a: pl.BlockSpec(a.shape, lambda b, h, t: (0, 0, 0))
    return pl.pallas_call(
        functools.partial(_hgrn_scan_kernel, layer=layer, tb=tb, hb=hb),
        grid=(n_batch, H // hb, nt),
        in_specs=[sec(0), sec(1), sec(2), sec(3),
                  pl.BlockSpec((lb_raw.shape[0], W), lambda b, h, t: (0, h)),
                  pl.BlockSpec((1, W), lambda b, h, t: (0, h)),
                  const(seg), const(pm), const(rm)],
        out_specs=[pl.BlockSpec((tb, W), lambda b, h, t: (b * nt + t, h)),
                   pl.BlockSpec((1, hb, B_DK, B_DV), lambda b, h, t: (b, h, 0, 0))],
        out_shape=[jax.ShapeDtypeStruct((n_batch * seq, H * B_DV), BF16),
                   jax.ShapeDtypeStruct((n_batch, H, B_DK, B_DV), F32)],
        scratch_shapes=[pltpu.VMEM((hb, B_DV, B_DK), F32), pltpu.VMEM((hb, tb, tb), F32)],
        compiler_params=_cparams(("parallel", "parallel", "arbitrary")),
        name="hgrn_scan",
    )(proj, proj, proj, proj, lb_raw, g_o.reshape(1, -1), seg, pm, rm)


def _hgrn_step_kernel(p_ref, s_ref, lbr_ref, go_ref, o_ref, so_ref, *, layer):
    H = B_HEADS
    p = p_ref[0]
    lb = _lower_bound(lbr_ref[...], layer)[0]
    gate = lb + (1.0 - lb) * _sigmoid(p[H:2 * H])
    q = _silu(p[0:H])
    v = p[2 * H:3 * H]
    og = p[3 * H:4 * H]
    packed = jnp.concatenate([gate, q, jnp.zeros((128 - 2 * H, B_DK), F32)], axis=0)
    cols = packed.T
    outs = []
    for h in range(H):
        g_col = cols[:, h:h + 1]
        q_col = cols[:, H + h:H + h + 1]
        s_new = g_col * s_ref[0, h] + (1.0 - g_col) * v[h:h + 1, :]
        so_ref[0, h] = s_new
        outs.append(jnp.sum(q_col * s_new, axis=0, keepdims=True))
    o = jnp.concatenate(outs, axis=0)
    o_ref[0] = (_rms(o, go_ref[...]) * _silu(og)).astype(o_ref.dtype)


def hgrn_step(proj_s, state, lb_raw, g_o, layer):
    Bd = proj_s.shape[0]
    H = B_HEADS
    depth = lb_raw.shape[0]
    o, s_new = pl.pallas_call(
        functools.partial(_hgrn_step_kernel, layer=layer),
        grid=(Bd,),
        in_specs=[pl.BlockSpec((1, 4 * H, 128), lambda r: (r, 0, 0)),
                  pl.BlockSpec((1, H, B_DK, B_DV), lambda r: (r, 0, 0, 0)),
                  pl.BlockSpec((depth, H, B_DK), lambda r: (0, 0, 0)),
                  pl.BlockSpec((H, B_DV), lambda r: (0, 0))],
        out_specs=[pl.BlockSpec((1, H, B_DV), lambda r: (r, 0, 0)),
                   pl.BlockSpec((1, H, B_DK, B_DV), lambda r: (r, 0, 0, 0))],
        out_shape=[jax.ShapeDtypeStruct((Bd, H, B_DV), BF16),
                   jax.ShapeDtypeStruct(state.shape, F32)],
        compiler_params=_cparams(("parallel",)),
        name="hgrn_step",
    )(proj_s.reshape(Bd, 4 * H, 128), state, lb_raw.reshape(depth, H, B_DK), g_o.reshape(H, B_DV))
    return o.reshape(Bd, H * B_DV), s_new


def hgrn_layer(x, g_mix, w_in_all, lb_raw, g_o, w_o_all, state, j, layer, n_batch, seq, scan_tb):
    Mp = n_batch * seq
    proj = norm_mm(x, g_mix, w_in_all, (0,), w_in_all.shape[-1], 512, lambda a: a, F32, "hgrn_in", layer=j)
    o_p, st_p = hgrn_scan(proj, lb_raw, g_o, layer, n_batch, seq, scan_tb)
    o_s, st_s = hgrn_step(proj[Mp:], state, lb_raw, g_o, layer)
    o = jnp.concatenate([o_p, o_s], axis=0)
    x = mm(o, w_o_all, F32, "hgrn_out", res=x, scale=1.0, layer=j, tn=512)
    return x, st_p, st_s


def _gmlp_mix_kernel(u_ref, v_ref, gv_ref, ws_ref, bias_ref, vo_ref, z_ref):
    v = _rms(v_ref[...], gv_ref[...])
    vo_ref[...] = v
    vb = v.astype(BF16)
    L = v.shape[0]
    r_id = lax.broadcasted_iota(jnp.int32, (L, L), 0)
    c_id = lax.broadcasted_iota(jnp.int32, (L, L), 1)
    for g in range(C_GROUPS):
        sl = slice(g * C_GDIM, (g + 1) * C_GDIM)
        w = jnp.where(r_id >= c_id, ws_ref[g], 0.0).astype(BF16)
        mixed = jnp.dot(w, vb[:, sl], preferred_element_type=F32) + bias_ref[:, sl]
        z_ref[:, sl] = (u_ref[:, sl] * mixed).astype(z_ref.dtype)


def gmlp_mix(uv, g_v, w_s, bias_full, n_rows):
    W = g_v.shape[0]
    L = C_CHUNK
    return pl.pallas_call(
        _gmlp_mix_kernel,
        grid=(n_rows // L,),
        in_specs=[pl.BlockSpec((L, W), lambda c: (c, 0)),
                  pl.BlockSpec((L, W), lambda c: (c, 1)),
                  pl.BlockSpec((1, W), lambda c: (0, 0)),
                  pl.BlockSpec((C_GROUPS, L, L), lambda c: (0, 0, 0)),
                  pl.BlockSpec((L, W), lambda c: (0, 0))],
        out_specs=[pl.BlockSpec((L, W), lambda c: (c, 0)),
                   pl.BlockSpec((L, W), lambda c: (c, 0))],
        out_shape=[jax.ShapeDtypeStruct((n_rows, W), F32),
                   jax.ShapeDtypeStruct((n_rows, W), BF16)],
        compiler_params=_cparams(("parallel",)),
        name="gmlp_mix",
    )(uv, uv, g_v.reshape(1, W), w_s, bias_full)


def _gmlp_single_kernel(u_ref, v_ref, gv_ref, w0_ref, b0_ref, vo_ref, z_ref):
    v = _rms(v_ref[...], gv_ref[...])
    vo_ref[...] = v
    z_ref[...] = (u_ref[...] * (w0_ref[...] * v + b0_ref[...])).astype(z_ref.dtype)


def gmlp_single(uv_s, g_v, w0, b0):
    R = uv_s.shape[0]
    W = g_v.shape[0]
    blk = lambda j: pl.BlockSpec((R, W), lambda i, j=j: (0, j))
    vec = pl.BlockSpec((1, W), lambda i: (0, 0))
    return pl.pallas_call(
        _gmlp_single_kernel,
        grid=(1,),
        in_specs=[blk(0), blk(1), vec, vec, vec],
        out_specs=[blk(0), blk(0)],
        out_shape=[jax.ShapeDtypeStruct((R, W), F32), jax.ShapeDtypeStruct((R, W), BF16)],
        compiler_params=_cparams(("arbitrary",)),
        name="gmlp_single",
    )(uv_s, uv_s, g_v.reshape(1, W), w0, b0)


def gmlp_layer(x, g_mix, w_in_all, g_v, w_s, b_s, w_o_all, j, n_batch, seq):
    Mp = n_batch * seq
    W = g_v.shape[0]
    uv = norm_mm(x, g_mix, w_in_all, (0,), w_in_all.shape[-1], 512, _gelu_tanh, F32, "gmlp_in", layer=j)
    assert seq % C_CHUNK == 0
    bias_full = jnp.repeat(b_s[:, :C_CHUNK].T, C_GDIM, axis=1)
    v_p, z_p = gmlp_mix(uv, g_v, w_s[:, :C_CHUNK, :C_CHUNK], bias_full, Mp)
    w0 = jnp.repeat(w_s[:, 0, 0], C_GDIM).reshape(1, W)
    b0 = jnp.repeat(b_s[:, 0], C_GDIM).reshape(1, W)
    v_s, z_s = gmlp_single(uv[Mp:], g_v, w0, b0)
    z = jnp.concatenate([z_p, z_s], axis=0)
    x = mm(z, w_o_all, F32, "gmlp_out", res=x, scale=1.0, layer=j, tn=512)
    last = ((seq - 1) // C_CHUNK) * C_CHUNK
    v_last = v_p.reshape(n_batch, seq, W)[:, last:]
    return x, v_last, v_s


def _forward(x_prompt, x_sample, cache_a_ckv, cache_a_kr, state_b, page_table, norm_ffn_a,
             ffn_a_wi, ffn_a_wo, norm_mix, a_w_in, a_g_q, a_g_kv, a_w_uq, a_w_ukv, a_w_o,
             b_w_in, b_lower_bounds, b_g_o, b_w_o, c_w_in, c_g_v, c_w_s, c_b_s, c_w_o,
             norm_ffn_b, ffn_b_wi, ffn_b_wo, final_norm, *, flash_tile, scan_tb):
    n_batch, seq, D = x_prompt.shape
    Bd, dec_seq, _ = x_sample.shape
    assert dec_seq == 1
    depth = norm_mix.shape[0]
    Mp = n_batch * seq
    past_len = page_table.shape[1] * PAGE_SIZE
    x = jnp.concatenate([x_prompt.reshape(Mp, D), x_sample.reshape(Bd, D)], axis=0)

    pos = jnp.concatenate([jnp.tile(jnp.arange(seq), n_batch),
                           jnp.full((Bd,), past_len, jnp.int32)])
    cos_t, sin_t = _rope_tables(pos)

    cache_krt = jnp.swapaxes(cache_a_kr, 2, 3)

    a_ckv, a_kr, b_sp, b_ss, c_vp, c_vs = [], [], [], [], [], []
    for i in range(depth):
        x = ffn_half_step(x, norm_ffn_a[i], ffn_a_wi, ffn_a_wo, i)
        j = i // N_MIXERS
        kind = i % N_MIXERS
        if kind == 0:
            x, ckv, kr = mla_layer(x, norm_mix[i], a_w_in[j], a_g_q[j], a_g_kv[j], a_w_uq[j],
                                   a_w_ukv[j], a_w_o, cache_a_ckv, cache_krt, page_table,
                                   j, n_batch, seq, cos_t, sin_t, flash_tile)
            a_ckv.append(ckv)
            a_kr.append(kr)
        elif kind == 1:
            x, sp, ss = hgrn_layer(x, norm_mix[i], b_w_in, b_lower_bounds, b_g_o[j], b_w_o,
                                   state_b[j], j, i, n_batch, seq, scan_tb)
            b_sp.append(sp)
            b_ss.append(ss)
        else:
            x, vp, vs = gmlp_layer(x, norm_mix[i], c_w_in, c_g_v[j], c_w_s[j], c_b_s[j],
                                   c_w_o, j, n_batch, seq)
            c_vp.append(vp)
            c_vs.append(vs)
        x = ffn_half_step(x, norm_ffn_b[i], ffn_b_wi, ffn_b_wo, i)

    y_p = final_norm_call(x, final_norm, 0, Mp, min(1024, Mp))
    y_s = final_norm_call(x, final_norm, Mp, Bd, Bd)
    ckv_all = jnp.stack(a_ckv)
    kr_all = jnp.stack(a_kr)
    return (y_p.reshape(n_batch, seq, D), y_s.reshape(Bd, 1, D),
            ckv_all[:, :Mp].reshape(-1, n_batch, seq, A_KV_LORA),
            kr_all[:, :Mp].reshape(-1, n_batch, seq, A_ROPE),
            ckv_all[:, Mp:].reshape(-1, Bd, 1, A_KV_LORA),
            kr_all[:, Mp:].reshape(-1, Bd, 1, A_ROPE),
            jnp.stack(b_sp), jnp.stack(b_ss),
            jnp.stack(c_vp), jnp.stack(c_vs)[:, :, None, :])


def kernel(x_prompt, x_sample, cache_a_ckv, cache_a_kr, state_b, page_table, norm_ffn_a, ffn_a_wi, ffn_a_wo, norm_mix, a_w_in, a_g_q, a_g_kv, a_w_uq, a_w_ukv, a_w_o, b_w_in, b_lower_bounds, b_g_o, b_w_o, c_w_in, c_g_v, c_w_s, c_b_s, c_w_o, norm_ffn_b, ffn_b_wi, ffn_b_wo, final_norm):
    seq = x_prompt.shape[1]
    return _forward(x_prompt, x_sample, cache_a_ckv, cache_a_kr, state_b, page_table, norm_ffn_a,
                    ffn_a_wi, ffn_a_wo, norm_mix, a_w_in, a_g_q, a_g_kv, a_w_uq, a_w_ukv, a_w_o,
                    b_w_in, b_lower_bounds, b_g_o, b_w_o, c_w_in, c_g_v, c_w_s, c_b_s, c_w_o,
                    norm_ffn_b, ffn_b_wi, ffn_b_wo, final_norm,
                    flash_tile=min(512, seq // 2), scan_tb=min(256, seq))
```

```python
import functools
import math

import jax
import jax.numpy as jnp
from jax import lax
from jax.experimental import pallas as pl
from jax.experimental.pallas import tpu as pltpu

F32 = jnp.float32
BF16 = jnp.bfloat16

EPS = 1e-6
ROPE_THETA = 10000.0
N_MIXERS = 3

A_HEADS = 16
A_NOPE = 128
A_ROPE = 64
A_VDIM = 128
A_Q_LORA = 512
A_KV_LORA = 512
A_QK_PAD = 256
Q_PRESCALE = (A_NOPE + A_ROPE) ** -0.5 * math.log2(math.e)
PAGE_SIZE = 128
N_DECODE_SLOTS = 4

B_HEADS = 16
B_DK = 128
B_DV = 128

C_GROUPS = 16
C_GDIM = 128
C_CHUNK = 128

MIB = 1024 * 1024


def _cparams(semantics, vmem_mib=48):
    return pltpu.CompilerParams(dimension_semantics=semantics,
                                vmem_limit_bytes=vmem_mib * MIB)


def _rms(x, g):
    return x * lax.rsqrt(jnp.mean(x * x, axis=-1, keepdims=True) + EPS) * g


def _sigmoid(x):
    return 1.0 / (1.0 + jnp.exp(-x))


def _silu(x):
    return x * _sigmoid(x)


def _gelu_tanh(x):
    c = math.sqrt(2.0 / math.pi)
    return 0.5 * x * (1.0 + jnp.tanh(c * (x + 0.044715 * (x * x * x))))


def _row_tile(m, cap):
    best = None
    for t in range(16, cap + 1, 16):
        if m % t == 0:
            best = t
    assert best is not None, (m, cap)
    return best


def _norm_mm_kernel(x_ref, g_ref, *refs, n_w, epilogue):
    w_refs = refs[:n_w]
    o_ref = refs[n_w]
    h_ref = refs[n_w + 1]

    @pl.when(pl.program_id(1) == 0)
    def _():
        h_ref[...] = _rms(x_ref[...], g_ref[...]).astype(BF16)

    h = h_ref[...]
    accs = [jnp.dot(h, w[...].astype(BF16), preferred_element_type=F32) for w in w_refs]
    o_ref[...] = epilogue(*accs).astype(o_ref.dtype)


def _weight_spec(w, layer, rows, tn, col_block):
    if layer is None:
        assert w.ndim == 2
        return pl.BlockSpec((rows, tn), lambda i, j: (0, col_block(i, j)))
    assert w.ndim == 3
    return pl.BlockSpec((None, rows, tn), lambda i, j: (layer, 0, col_block(i, j)))


def norm_mm(x, g, w, col_offsets, out_cols, tn, epilogue, out_dtype, name, layer=None, tm_cap=1040):
    M, D = x.shape
    tm = _row_tile(M, tm_cap)
    n_w = len(col_offsets)
    in_specs = [pl.BlockSpec((tm, D), lambda i, j: (i, 0)),
                pl.BlockSpec((1, D), lambda i, j: (0, 0))]
    for off in col_offsets:
        assert off % tn == 0
        in_specs.append(_weight_spec(w, layer, D, tn, lambda i, j, o=off // tn: j + o))
    return pl.pallas_call(
        functools.partial(_norm_mm_kernel, n_w=n_w, epilogue=epilogue),
        grid=(M // tm, out_cols // tn),
        in_specs=in_specs,
        out_specs=pl.BlockSpec((tm, tn), lambda i, j: (i, j)),
        out_shape=jax.ShapeDtypeStruct((M, out_cols), out_dtype),
        scratch_shapes=[pltpu.VMEM((tm, D), BF16)],
        compiler_params=_cparams(("parallel", "arbitrary")),
        name=name,
    )(x, g.reshape(1, D), *([w] * n_w))


def _mm_kernel(a_ref, w_ref, *refs, scale, has_res):
    o_ref = refs[-1]
    acc = jnp.dot(a_ref[...], w_ref[...].astype(BF16), preferred_element_type=F32)
    if has_res:
        acc = refs[0][...] + scale * acc
    o_ref[...] = acc.astype(o_ref.dtype)


def mm(a, w, out_dtype, name, res=None, scale=1.0, layer=None, tm_cap=1040, tn=512, vmem_mib=48):
    M, K = a.shape
    N = w.shape[-1]
    tm = _row_tile(M, tm_cap)
    tn = min(tn, N)
    in_specs = [pl.BlockSpec((tm, K), lambda i, j: (i, 0)),
                _weight_spec(w, layer, K, tn, lambda i, j: j)]
    args = [a, w]
    if res is not None:
        in_specs.append(pl.BlockSpec((tm, tn), lambda i, j: (i, j)))
        args.append(res)
    return pl.pallas_call(
        functools.partial(_mm_kernel, scale=scale, has_res=res is not None),
        grid=(M // tm, N // tn),
        in_specs=in_specs,
        out_specs=pl.BlockSpec((tm, tn), lambda i, j: (i, j)),
        out_shape=jax.ShapeDtypeStruct((M, N), out_dtype),
        compiler_params=_cparams(("parallel", "parallel"), vmem_mib),
        name=name,
    )(*args)


def _head_mm_kernel(x_ref, w_ref, o_ref, *, heads, k, n):
    for h in range(heads):
        o_ref[:, h * n:(h + 1) * n] = jnp.dot(
            x_ref[:, h * k:(h + 1) * k], w_ref[h], preferred_element_type=F32).astype(o_ref.dtype)


def head_mm(x, w, out_dtype, name):
    R = x.shape[0]
    H, K, N = w.shape
    return pl.pallas_call(
        functools.partial(_head_mm_kernel, heads=H, k=K, n=N),
        grid=(1,),
        in_specs=[pl.BlockSpec((R, H * K), lambda i: (0, 0)),
                  pl.BlockSpec((H, K, N), lambda i: (0, 0, 0))],
        out_specs=pl.BlockSpec((R, H * N), lambda i: (0, 0)),
        out_shape=jax.ShapeDtypeStruct((R, H * N), out_dtype),
        compiler_params=_cparams(("arbitrary",)),
        name=name,
    )(x, w)


def _norm_kernel(x_ref, g_ref, o_ref):
    o_ref[...] = _rms(x_ref[...], g_ref[...])


def final_norm_call(x, g, row0, n_rows, tm):
    D = x.shape[1]
    assert row0 % tm == 0 and n_rows % tm == 0
    blk0 = row0 // tm
    return pl.pallas_call(
        _norm_kernel,
        grid=(n_rows // tm,),
        in_specs=[pl.BlockSpec((tm, D), lambda i: (i + blk0, 0)),
                  pl.BlockSpec((1, D), lambda i: (0, 0))],
        out_specs=pl.BlockSpec((tm, D), lambda i: (i, 0)),
        out_shape=jax.ShapeDtypeStruct((n_rows, D), F32),
        compiler_params=_cparams(("parallel",)),
        name="final_norm",
    )(x, g.reshape(1, D))


def _swiglu_epilogue(gate, up):
    return _silu(gate) * up


def ffn_half_step(x, g, wi, wo, layer):
    d_ff = wo.shape[1]
    hid = norm_mm(x, g, wi, (0, d_ff), d_ff, 512, _swiglu_epilogue, BF16, "ffn_in", layer=layer)
    return mm(hid, wo, F32, "ffn_out", res=x, scale=0.5, layer=layer, tm_cap=832, tn=512, vmem_mib=56)


def _mla_in_kernel(x_ref, g_ref, w_ref, gq_ref, gkv_ref, cos_ref, sin_ref,
                   cq_ref, ckv_ref, ckvb_ref, kr_ref, krb_ref):
    h = _rms(x_ref[...], g_ref[...]).astype(BF16)
    acc = jnp.dot(h, w_ref[...], preferred_element_type=F32)
    ql, kl = A_Q_LORA, A_KV_LORA
    cq_ref[...] = _rms(acc[:, :ql], gq_ref[...]).astype(BF16)
    ckv = _rms(acc[:, ql:ql + kl], gkv_ref[...])
    ckv_ref[...] = ckv
    ckvb_ref[...] = ckv.astype(BF16)
    kr = (acc[:, ql + kl:ql + kl + 128] * cos_ref[...]
          + acc[:, ql + kl + 128:ql + kl + 256] * sin_ref[...])
    kr_ref[...] = kr
    krb_ref[...] = kr.astype(BF16)


def mla_in(x, g, w_ext, g_q, g_kv, cos_t, sin_t):
    M, D = x.shape
    tm = _row_tile(M, 640)
    NW = w_ext.shape[1]
    row = lambda n: pl.BlockSpec((tm, n), lambda i: (i, 0))
    full = lambda a, b: pl.BlockSpec((a, b), lambda i: (0, 0))
    return pl.pallas_call(
        _mla_in_kernel,
        grid=(M // tm,),
        in_specs=[row(D), full(1, D), full(D, NW), full(1, A_Q_LORA), full(1, A_KV_LORA),
                  row(128), row(128)],
        out_specs=[row(A_Q_LORA), row(A_KV_LORA), row(A_KV_LORA), row(128), row(128)],
        out_shape=[jax.ShapeDtypeStruct((M, A_Q_LORA), BF16),
                   jax.ShapeDtypeStruct((M, A_KV_LORA), F32),
                   jax.ShapeDtypeStruct((M, A_KV_LORA), BF16),
                   jax.ShapeDtypeStruct((M, 128), F32),
                   jax.ShapeDtypeStruct((M, 128), BF16)],
        compiler_params=_cparams(("parallel",)),
        name="mla_in",
    )(x, g.reshape(1, D), w_ext, g_q.reshape(1, -1), g_kv.reshape(1, -1), cos_t, sin_t)


def _mla_q_kernel(cq_ref, w1_ref, w2_ref, cos_ref, sin_ref, o_ref):
    cq = cq_ref[...]
    cos = cos_ref[...]
    sin = sin_ref[...]
    P = A_QK_PAD
    for h in range(A_HEADS):
        a1 = jnp.dot(cq, w1_ref[:, h * P:(h + 1) * P], preferred_element_type=F32)
        a2 = jnp.dot(cq, w2_ref[:, h * 128:(h + 1) * 128], preferred_element_type=F32)
        o_ref[:, h * P:h * P + 128] = (a1[:, :128] * Q_PRESCALE).astype(BF16)
        o_ref[:, h * P + 128:(h + 1) * P] = ((a1[:, 128:] * cos + a2 * sin) * Q_PRESCALE).astype(BF16)


def mla_q(cq, w1, w2, cos_t, sin_t):
    M = cq.shape[0]
    tm = _row_tile(M, 640)
    row = lambda n: pl.BlockSpec((tm, n), lambda i: (i, 0))
    full = lambda a, b: pl.BlockSpec((a, b), lambda i: (0, 0))
    return pl.pallas_call(
        _mla_q_kernel,
        grid=(M // tm,),
        in_specs=[row(A_Q_LORA), full(*w1.shape), full(*w2.shape), row(128), row(128)],
        out_specs=row(A_HEADS * A_QK_PAD),
        out_shape=jax.ShapeDtypeStruct((M, A_HEADS * A_QK_PAD), BF16),
        compiler_params=_cparams(("parallel",)),
        name="mla_q",
    )(cq, w1, w2, cos_t, sin_t)


def _mla_kv_kernel(c_ref, wk_ref, wvt_ref, kn_ref, vt_ref):
    c = c_ref[...]
    kn_ref[...] = jnp.dot(c, wk_ref[...], preferred_element_type=F32).astype(BF16)
    vt_ref[...] = lax.dot_general(wvt_ref[...], c, (((1,), (1,)), ((), ())),
                                  preferred_element_type=F32).astype(BF16)


def mla_kv(ckvb, w_uk_cols, w_uv_rows, n_rows, tm):
    C = A_KV_LORA
    N = w_uk_cols.shape[1]
    NV = w_uv_rows.shape[0]
    return pl.pallas_call(
        _mla_kv_kernel,
        grid=(n_rows // tm,),
        in_specs=[pl.BlockSpec((tm, C), lambda i: (i, 0)),
                  pl.BlockSpec((C, N), lambda i: (0, 0)),
                  pl.BlockSpec((N, C), lambda i: (0, 0))],
        out_specs=[pl.BlockSpec((tm, N), lambda i: (i, 0)),
                   pl.BlockSpec((NV, tm), lambda i: (0, i))],
        out_shape=[jax.ShapeDtypeStruct((n_rows, N), BF16),
                   jax.ShapeDtypeStruct((NV, n_rows), BF16)],
        compiler_params=_cparams(("parallel",)),
        name="mla_kv",
    )(ckvb, w_uk_cols, w_uv_rows)


def _flash_kernel(q_ref, kn_ref, kr_ref, vt_ref, o_ref, m_ref, l_ref, acc_ref, *, half, n_pairs):
    nt_dims = (((1,), (1,)), ((), ()))

    def update(j, q0, k0, klen, masked):
        q = q_ref[pl.ds(q0, half), :]
        k = jnp.concatenate([kn_ref[pl.ds(k0, klen), :], kr_ref[pl.ds(k0, klen), :]], axis=1)
        st = lax.dot_general(k, q, nt_dims, preferred_element_type=F32)
        if masked:
            kpos = lax.broadcasted_iota(jnp.int32, st.shape, 0)
            qpos = lax.broadcasted_iota(jnp.int32, st.shape, 1)
            st = jnp.where(kpos <= qpos, st, -jnp.inf)
        m_prev = m_ref[j]
        m_new = jnp.maximum(m_prev, jnp.max(st, axis=0, keepdims=True))
        alpha = jnp.exp2(m_prev - m_new)
        p = jnp.exp2(st - m_new)
        l_ref[j] = alpha * l_ref[j] + jnp.sum(p, axis=0, keepdims=True)
        acc_ref[j] = alpha * acc_ref[j] + jnp.dot(vt_ref[:, pl.ds(k0, klen)], p.astype(BF16),
                                                  preferred_element_type=F32)
        m_ref[j] = m_new

    def finish(j):
        return (acc_ref[j] / l_ref[j]).T.astype(o_ref.dtype)

    def pair(p, carry):
        qa = pl.multiple_of(p * (2 * half), 2 * half)
        qb = pl.multiple_of(qa + half, half)
        m_ref[...] = jnp.full(m_ref.shape, -jnp.inf, F32)
        l_ref[...] = jnp.zeros(l_ref.shape, F32)
        acc_ref[...] = jnp.zeros(acc_ref.shape, F32)

        def k_step(kc, c):
            k0 = pl.multiple_of(kc * (2 * half), 2 * half)
            update(0, qa, k0, 2 * half, False)
            update(1, qb, k0, 2 * half, False)
            return c

        lax.fori_loop(0, p, k_step, 0)
        update(0, qa, qa, half, True)
        update(1, qb, qa, half, False)
        update(1, qb, qb, half, True)
        o_ref[pl.ds(qa, half), :] = finish(0)
        o_ref[pl.ds(qb, half), :] = finish(1)
        return carry

    lax.fori_loop(0, n_pairs, pair, 0)


def mla_flash(qp, kn, krb, vt, n_batch, seq, half):
    H = A_HEADS
    assert seq % (2 * half) == 0
    return pl.pallas_call(
        functools.partial(_flash_kernel, half=half, n_pairs=seq // (2 * half)),
        grid=(n_batch, H),
        in_specs=[pl.BlockSpec((seq, A_QK_PAD), lambda b, h: (b, h)),
                  pl.BlockSpec((seq, A_NOPE), lambda b, h: (b, h)),
                  pl.BlockSpec((seq, 128), lambda b, h: (b, 0)),
                  pl.BlockSpec((A_VDIM, seq), lambda b, h: (h, b))],
        out_specs=pl.BlockSpec((seq, A_VDIM), lambda b, h: (b, h)),
        out_shape=jax.ShapeDtypeStruct((n_batch * seq, H * A_VDIM), BF16),
        scratch_shapes=[pltpu.VMEM((2, 1, half), F32), pltpu.VMEM((2, 1, half), F32),
                        pltpu.VMEM((2, A_VDIM, half), F32)],
        compiler_params=_cparams(("parallel", "parallel")),
        name="mla_flash",
    )(qp, kn, krb, vt)


def _decode_kernel(pt_ref, ql_ref, qr_ref, cn_ref, rn_ref, ckv_hbm, krt_hbm, o_ref,
                   cbuf, rbuf, sems, m_ref, l_ref, acc_ref,
                   *, layer, n_pages, chunk, group):
    g = pl.program_id(0)
    n_chunks = n_pages // chunk
    last = 2 * pl.num_programs(0) - 1
    c0 = (2 * g) % n_chunks

    def copies(k, slot):
        out = []
        rg = k // n_chunks
        cc = k % n_chunks
        for r in range(group):
            base = (rg * group + r) * n_pages + cc * chunk
            for p in range(chunk):
                page = pt_ref[base + p]
                out.append(pltpu.make_async_copy(ckv_hbm.at[layer, page], cbuf.at[slot, r, p],
                                                 sems.at[0, slot]))
                out.append(pltpu.make_async_copy(krt_hbm.at[layer, page], rbuf.at[slot, r, p],
                                                 sems.at[1, slot]))
        return out

    def start(k, slot):
        for cp in copies(k, slot):
            cp.start()

    def wait(k, slot):
        for cp in copies(k, slot):
            cp.wait()

    s0 = (2 * g) % N_DECODE_SLOTS

    @pl.when(g == 0)
    def _():
        start(0, 0)
        start(1, 1)

    @pl.when(c0 == 0)
    def _():
        for r in range(group):
            cn = cn_ref[r].astype(F32)
            rn = rn_ref[r].astype(F32)
            s_new = (jnp.sum(ql_ref[r].astype(F32) * cn, axis=-1, keepdims=True)
                     + jnp.sum(qr_ref[r].astype(F32) * rn, axis=-1, keepdims=True))
            m_ref[r] = s_new
            l_ref[r] = jnp.ones(s_new.shape, F32)
            acc_ref[r] = jnp.broadcast_to(cn, acc_ref.shape[1:])

    nt_dims = (((1,), (1,)), ((), ()))

    def attend(slot):
        cbs, scores = [], []
        for r in range(group):
            cb = cbuf[slot, r].reshape(chunk * PAGE_SIZE, A_KV_LORA).astype(BF16)
            s_rope = jnp.concatenate(
                [jnp.dot(qr_ref[r], rbuf[slot, r, p].astype(BF16), preferred_element_type=F32)
                 for p in range(chunk)], axis=1)
            cbs.append(cb)
            scores.append(lax.dot_general(ql_ref[r], cb, nt_dims, preferred_element_type=F32) + s_rope)
        for r in range(group):
            s = scores[r]
            m_prev = m_ref[r]
            m_new = jnp.maximum(m_prev, jnp.max(s, axis=-1, keepdims=True))
            alpha = jnp.exp2(m_prev - m_new)
            p = jnp.exp2(s - m_new)
            l_ref[r] = alpha * l_ref[r] + jnp.sum(p, axis=-1, keepdims=True)
            acc_ref[r] = alpha * acc_ref[r] + jnp.dot(p.astype(BF16), cbs[r], preferred_element_type=F32)
            m_ref[r] = m_new

    start(jnp.minimum(2 * g + 2, last), (s0 + 2) % N_DECODE_SLOTS)
    wait(2 * g, s0)
    attend(s0)
    start(jnp.minimum(2 * g + 3, last), (s0 + 3) % N_DECODE_SLOTS)
    wait(2 * g + 1, s0 + 1)
    attend(s0 + 1)

    @pl.when(c0 + 1 == n_chunks - 1)
    def _():
        for r in range(group):
            o_ref[r] = (acc_ref[r] / l_ref[r]).astype(o_ref.dtype)

    @pl.when(2 * g + 1 == last)
    def _():
        wait(last, (s0 + 2) % N_DECODE_SLOTS)
        wait(last, (s0 + 3) % N_DECODE_SLOTS)


def mla_decode(page_table, ql, qr, cn, rn, cache_ckv, cache_krt, layer, chunk=8, group=2):
    Bd, n_pages = page_table.shape
    assert n_pages % (2 * chunk) == 0 and Bd % group == 0
    steps_per_group = n_pages // (2 * chunk)
    H, C, R = A_HEADS, A_KV_LORA, A_ROPE
    req = lambda shp: pl.BlockSpec((group,) + shp, lambda g, pt: (g // steps_per_group, 0, 0))
    grid_spec = pltpu.PrefetchScalarGridSpec(
        num_scalar_prefetch=1,
        grid=(Bd // group * steps_per_group,),
        in_specs=[req((H, C)), req((H, R)), req((1, C)), req((1, R)),
                  pl.BlockSpec(memory_space=pl.ANY), pl.BlockSpec(memory_space=pl.ANY)],
        out_specs=req((H, C)),
        scratch_shapes=[pltpu.VMEM((N_DECODE_SLOTS, group, chunk, PAGE_SIZE, C), F32),
                        pltpu.VMEM((N_DECODE_SLOTS, group, chunk, R, PAGE_SIZE), F32),
                        pltpu.SemaphoreType.DMA((2, N_DECODE_SLOTS)),
                        pltpu.VMEM((group, H, 1), F32), pltpu.VMEM((group, H, 1), F32),
                        pltpu.VMEM((group, H, C), F32)])
    return pl.pallas_call(
        functools.partial(_decode_kernel, layer=layer, n_pages=n_pages, chunk=chunk, group=group),
        grid_spec=grid_spec,
        out_shape=jax.ShapeDtypeStruct((Bd, H, C), BF16),
        compiler_params=_cparams(("arbitrary",)),
        name="mla_decode",
    )(page_table.reshape(-1), ql, qr, cn, rn, cache_ckv, cache_krt)


def _rope_tables(pos):
    half = A_ROPE // 2
    inv = ROPE_THETA ** (-jnp.arange(half, dtype=F32) / half)
    ang = pos.astype(F32)[:, None] * inv[None, :]
    cos, sin = jnp.cos(ang), jnp.sin(ang)
    z = jnp.zeros((pos.shape[0], 128 - A_ROPE), F32)
    return (jnp.concatenate([cos, cos, z], axis=1), jnp.concatenate([sin, sin, z], axis=1))


def _half_swap(w):
    half = A_ROPE // 2
    return jnp.concatenate([-w[..., half:], w[..., :half]], axis=-1)


def mla_layer(x, g_mix, w_in, g_q, g_kv, w_uq, w_ukv, w_o_all, cache_ckv, cache_krt, page_table,
              layer, n_batch, seq, cos_t, sin_t, flash_tile):
    D = x.shape[1]
    Mp = n_batch * seq
    H = A_HEADS
    lo = A_Q_LORA + A_KV_LORA
    zpad = jnp.zeros((D, 128 - A_ROPE), F32)
    w_kr = w_in[:, lo:]
    w_in_ext = jnp.concatenate([w_in[:, :lo], w_kr, zpad, _half_swap(w_kr), zpad], axis=1).astype(BF16)
    cq, ckv, ckvb, kr, krb = mla_in(x, g_mix, w_in_ext, g_q, g_kv, cos_t, sin_t)

    wq = w_uq.reshape(A_Q_LORA, H, A_NOPE + A_ROPE)
    wq_rope = wq[..., A_NOPE:]
    zq = jnp.zeros((A_Q_LORA, H, 128 - A_ROPE), F32)
    w1 = jnp.concatenate([wq[..., :A_NOPE], wq_rope, zq], axis=-1).reshape(A_Q_LORA, H * A_QK_PAD)
    w2 = jnp.concatenate([_half_swap(wq_rope), zq], axis=-1).reshape(A_Q_LORA, H * 128)
    qp = mla_q(cq, w1.astype(BF16), w2.astype(BF16), cos_t, sin_t)

    wkv = w_ukv.reshape(A_KV_LORA, H, A_NOPE + A_VDIM)
    w_uk = wkv[..., :A_NOPE]
    w_uv = wkv[..., A_NOPE:]
    kn, vt = mla_kv(ckvb, w_uk.reshape(A_KV_LORA, H * A_NOPE).astype(BF16),
                    jnp.transpose(w_uv, (1, 2, 0)).reshape(H * A_VDIM, A_KV_LORA).astype(BF16),
                    Mp, min(1024, seq))
    o_p = mla_flash(qp, kn, krb, vt, n_batch, seq, flash_tile)

    Bd = page_table.shape[0]
    qs = qp[Mp:].reshape(Bd, H, A_QK_PAD)
    q_nope = qs[:, :, :A_NOPE].reshape(Bd, H * A_NOPE)
    q_rope = qs[:, :, A_NOPE:A_NOPE + A_ROPE]
    q_lat = head_mm(q_nope, jnp.transpose(w_uk, (1, 2, 0)).astype(BF16), BF16, "mla_qlat")
    o_lat = mla_decode(page_table, q_lat.reshape(Bd, H, A_KV_LORA), q_rope,
                       ckvb[Mp:].reshape(Bd, 1, A_KV_LORA),
                       krb[Mp:, :A_ROPE].reshape(Bd, 1, A_ROPE),
                       cache_ckv, cache_krt, layer)
    o_s = head_mm(o_lat.reshape(Bd, H * A_KV_LORA), jnp.transpose(w_uv, (1, 0, 2)).astype(BF16), BF16,
                  "mla_ouv")

    o = jnp.concatenate([o_p, o_s], axis=0)
    x = mm(o, w_o_all, F32, "mla_out", res=x, scale=1.0, layer=layer, tn=512)
    return x, ckv, kr[:, :A_ROPE]


def _lower_bound(raw, layer):
    e = jnp.exp(raw - jnp.max(raw, axis=0, keepdims=True))
    tot = jnp.sum(e, axis=0, keepdims=True)
    if layer == 0:
        return jnp.zeros_like(tot)
    return jnp.sum(e[1:layer + 1], axis=0, keepdims=True) / tot


def _hgrn_scan_kernel(q_ref, f_ref, i_ref, og_ref, lbr_ref, go_ref,
                      seg_ref, pm_ref, rm_ref, o_ref, st_ref, state_ref, a_ref, *, layer, tb, hb):
    t = pl.program_id(2)
    n_lev = pm_ref.shape[0]
    tn_dims = (((0,), (0,)), ((), ()))
    nt_dims = (((1,), (1,)), ((), ()))

    @pl.when(t == 0)
    def _():
        state_ref[...] = jnp.zeros(state_ref.shape, F32)

    lb = _lower_bound(lbr_ref[...], layer)
    gate = lb + (1.0 - lb) * _sigmoid(f_ref[...])
    logf = jnp.log(gate)
    q = _silu(q_ref[...])
    k = 1.0 - gate
    vb = i_ref[...].astype(BF16)
    a1 = logf.astype(BF16)
    r1 = logf - a1.astype(F32)
    a2 = r1.astype(BF16)
    a3 = (r1 - a2.astype(F32)).astype(BF16)

    def seg_sum(lev):
        m = seg_ref[lev]
        return (jnp.dot(m, a1, preferred_element_type=F32) + jnp.dot(m, a2, preferred_element_type=F32)
                + jnp.dot(m, a3, preferred_element_type=F32))

    b = seg_sum(n_lev)
    width = b.shape[1]

    for lev in range(n_lev):
        right = jnp.concatenate([rm_ref[lev]] * hb, axis=1)
        blk = tb >> lev
        if blk >= 8:
            b3 = b.reshape(tb // blk, blk, width)
            b_split = jnp.broadcast_to(b3[:, blk // 2 - 1:blk // 2, :], b3.shape).reshape(tb, width)
            x = jnp.where(right > 0.5, b - b_split, b_split - b)
        else:
            x = seg_sum(lev)
        e = jnp.exp(x)
        qh = (q * e * right).astype(BF16)
        kh = (k * e * (1.0 - right)).astype(BF16)
        for hh in range(hb):
            ls = slice(hh * B_DK, (hh + 1) * B_DK)
            s = lax.dot_general(qh[:, ls], kh[:, ls], nt_dims, preferred_element_type=F32) * pm_ref[lev]
            if lev == 0:
                a_ref[hh] = s
            else:
                a_ref[hh] = a_ref[hh] + s

    b_last = b[tb - 1:tb, :]
    qe = (q * jnp.exp(b)).astype(BF16)
    kd = (k * jnp.exp(b_last - b)).astype(BF16)
    decay = jnp.exp(b_last)
    qk = q * k
    for hh in range(hb):
        ls = slice(hh * B_DK, (hh + 1) * B_DK)
        st = state_ref[hh]
        v = vb[:, ls]
        o = (lax.dot_general(qe[:, ls], st.astype(BF16), nt_dims, preferred_element_type=F32)
             + jnp.dot(a_ref[hh].astype(BF16), v, preferred_element_type=F32)
             + jnp.sum(qk[:, ls], axis=-1, keepdims=True) * i_ref[:, ls])
        state_ref[hh] = st * decay[:, ls] + lax.dot_general(v, kd[:, ls], tn_dims,
                                                            preferred_element_type=F32)
        o = _rms(o, go_ref[:, ls]) * _silu(og_ref[:, ls])
        o_ref[:, ls] = o.astype(o_ref.dtype)

    @pl.when(t == pl.num_programs(2) - 1)
    def _():
        for hh in range(hb):
            st_ref[0, hh] = state_ref[hh].T


def _scan_constants(tb):
    assert tb & (tb - 1) == 0
    t = jnp.arange(tb)[:, None]
    j = jnp.arange(tb)[None, :]
    segs, pms, rms = [], [], []
    h = tb // 2
    while h >= 1:
        split = (t // (2 * h)) * (2 * h) + h - 1
        right = t > split
        segs.append(jnp.where(right, (j > split) & (j <= t), (j > t) & (j <= split)))
        pms.append((t // (2 * h)) == (j // (2 * h)))
        rms.append(jnp.broadcast_to(right, (tb, B_DK)))
        h //= 2
    segs.append(j <= t)
    return (jnp.stack(segs).astype(BF16), jnp.stack(pms).astype(F32), jnp.stack(rms).astype(F32))


def hgrn_scan(proj, lb_raw, g_o, layer, n_batch, seq, tb, hb=4):
    H = B_HEADS
    assert H % hb == 0
    nt = seq // tb
    W = hb * B_DK
    seg, pm, rm = _scan_constants(tb)
    sec = lambda k: pl.BlockSpec((tb, W), lambda b, h, t, k=k: (b * nt + t, k * (H // hb) + h))
    const = lambda a: pl.BlockSpec(a.shape, lambda b, h, t: (0, 0, 0))
    return pl.pallas_call(
        functools.partial(_hgrn_scan_kernel, layer=layer, tb=tb, hb=hb),
        grid=(n_batch, H // hb, nt),
        in_specs=[sec(0), sec(1), sec(2), sec(3),
                  pl.BlockSpec((lb_raw.shape[0], W), lambda b, h, t: (0, h)),
                  pl.BlockSpec((1, W), lambda b, h, t: (0, h)),
                  const(seg), const(pm), const(rm)],
        out_specs=[pl.BlockSpec((tb, W), lambda b, h, t: (b * nt + t, h)),
                   pl.BlockSpec((1, hb, B_DK, B_DV), lambda b, h, t: (b, h, 0, 0))],
        out_shape=[jax.ShapeDtypeStruct((n_batch * seq, H * B_DV), BF16),
                   jax.ShapeDtypeStruct((n_batch, H, B_DK, B_DV), F32)],
        scratch_shapes=[pltpu.VMEM((hb, B_DV, B_DK), F32), pltpu.VMEM((hb, tb, tb), F32)],
        compiler_params=_cparams(("parallel", "parallel", "arbitrary")),
        name="hgrn_scan",
    )(proj, proj, proj, proj, lb_raw, g_o.reshape(1, -1), seg, pm, rm)


def _hgrn_step_kernel(p_ref, s_ref, lbr_ref, go_ref, o_ref, so_ref, *, layer):
    H = B_HEADS
    p = p_ref[0]
    lb = _lower_bound(lbr_ref[...], layer)[0]
    gate = lb + (1.0 - lb) * _sigmoid(p[H:2 * H])
    q = _silu(p[0:H])
    v = p[2 * H:3 * H]
    og = p[3 * H:4 * H]
    packed = jnp.concatenate([gate, jnp.zeros((128 - H, B_DK), F32)], axis=0)
    cols = packed.T
    qb = q.astype(BF16)
    outs = []
    for h in range(H):
        g_col = cols[:, h:h + 1]
        s_new = g_col * s_ref[0, h] + (1.0 - g_col) * v[h:h + 1, :]
        so_ref[0, h] = s_new
        outs.append(jnp.dot(qb, s_new.astype(BF16), preferred_element_type=F32)[h:h + 1, :])
    o = jnp.concatenate(outs, axis=0)
    o_ref[0] = (_rms(o, go_ref[...]) * _silu(og)).astype(o_ref.dtype)


def hgrn_step(proj_s, state, lb_raw, g_o, layer):
    Bd = proj_s.shape[0]
    H = B_HEADS
    depth = lb_raw.shape[0]
    o, s_new = pl.pallas_call(
        functools.partial(_hgrn_step_kernel, layer=layer),
        grid=(Bd,),
        in_specs=[pl.BlockSpec((1, 4 * H, 128), lambda r: (r, 0, 0)),
                  pl.BlockSpec((1, H, B_DK, B_DV), lambda r: (r, 0, 0, 0)),
                  pl.BlockSpec((depth, H, B_DK), lambda r: (0, 0, 0)),
                  pl.BlockSpec((H, B_DV), lambda r: (0, 0))],
        out_specs=[pl.BlockSpec((1, H, B_DV), lambda r: (r, 0, 0)),
                   pl.BlockSpec((1, H, B_DK, B_DV), lambda r: (r, 0, 0, 0))],
        out_shape=[jax.ShapeDtypeStruct((Bd, H, B_DV), BF16),
                   jax.ShapeDtypeStruct(state.shape, F32)],
        compiler_params=_cparams(("parallel",)),
        name="hgrn_step",
    )(proj_s.reshape(Bd, 4 * H, 128), state, lb_raw.reshape(depth, H, B_DK), g_o.reshape(H, B_DV))
    return o.reshape(Bd, H * B_DV), s_new


def hgrn_layer(x, g_mix, w_in_all, lb_raw, g_o, w_o_all, state, j, layer, n_batch, seq, scan_tb):
    Mp = n_batch * seq
    proj = norm_mm(x, g_mix, w_in_all, (0,), w_in_all.shape[-1], 512, lambda a: a, F32, "hgrn_in", layer=j)
    o_p, st_p = hgrn_scan(proj, lb_raw, g_o, layer, n_batch, seq, scan_tb)
    o_s, st_s = hgrn_step(proj[Mp:], state, lb_raw, g_o, layer)
    o = jnp.concatenate([o_p, o_s], axis=0)
    x = mm(o, w_o_all, F32, "hgrn_out", res=x, scale=1.0, layer=j, tn=512)
    return x, st_p, st_s


def _gmlp_mix_kernel(u_ref, v_ref, gv_ref, ws_ref, bias_ref, vo_ref, z_ref):
    v = _rms(v_ref[...], gv_ref[...])
    vo_ref[...] = v
    vb = v.astype(BF16)
    L = v.shape[0]
    r_id = lax.broadcasted_iota(jnp.int32, (L, L), 0)
    c_id = lax.broadcasted_iota(jnp.int32, (L, L), 1)
    for g in range(C_GROUPS):
        sl = slice(g * C_GDIM, (g + 1) * C_GDIM)
        w = jnp.where(r_id >= c_id, ws_ref[g], 0.0).astype(BF16)
        mixed = jnp.dot(w, vb[:, sl], preferred_element_type=F32) + bias_ref[:, sl]
        z_ref[:, sl] = (u_ref[:, sl] * mixed).astype(z_ref.dtype)


def gmlp_mix(uv, g_v, w_s, bias_full, n_rows):
    W = g_v.shape[0]
    L = C_CHUNK
    return pl.pallas_call(
        _gmlp_mix_kernel,
        grid=(n_rows // L,),
        in_specs=[pl.BlockSpec((L, W), lambda c: (c, 0)),
                  pl.BlockSpec((L, W), lambda c: (c, 1)),
                  pl.BlockSpec((1, W), lambda c: (0, 0)),
                  pl.BlockSpec((C_GROUPS, L, L), lambda c: (0, 0, 0)),
                  pl.BlockSpec((L, W), lambda c: (0, 0))],
        out_specs=[pl.BlockSpec((L, W), lambda c: (c, 0)),
                   pl.BlockSpec((L, W), lambda c: (c, 0))],
        out_shape=[jax.ShapeDtypeStruct((n_rows, W), F32),
                   jax.ShapeDtypeStruct((n_rows, W), BF16)],
        compiler_params=_cparams(("parallel",)),
        name="gmlp_mix",
    )(uv, uv, g_v.reshape(1, W), w_s, bias_full)


def _gmlp_single_kernel(u_ref, v_ref, gv_ref, w0_ref, b0_ref, vo_ref, z_ref):
    v = _rms(v_ref[...], gv_ref[...])
    vo_ref[...] = v
    z_ref[...] = (u_ref[...] * (w0_ref[...] * v + b0_ref[...])).astype(z_ref.dtype)


def gmlp_single(uv_s, g_v, w0, b0):
    R = uv_s.shape[0]
    W = g_v.shape[0]
    blk = lambda j: pl.BlockSpec((R, W), lambda i, j=j: (0, j))
    vec = pl.BlockSpec((1, W), lambda i: (0, 0))
    return pl.pallas_call(
        _gmlp_single_kernel,
        grid=(1,),
        in_specs=[blk(0), blk(1), vec, vec, vec],
        out_specs=[blk(0), blk(0)],
        out_shape=[jax.ShapeDtypeStruct((R, W), F32), jax.ShapeDtypeStruct((R, W), BF16)],
        compiler_params=_cparams(("arbitrary",)),
        name="gmlp_single",
    )(uv_s, uv_s, g_v.reshape(1, W), w0, b0)


def gmlp_layer(x, g_mix, w_in_all, g_v, w_s, b_s, w_o_all, j, n_batch, seq):
    Mp = n_batch * seq
    W = g_v.shape[0]
    uv = norm_mm(x, g_mix, w_in_all, (0,), w_in_all.shape[-1], 512, _gelu_tanh, F32, "gmlp_in", layer=j)
    assert seq % C_CHUNK == 0
    bias_full = jnp.repeat(b_s[:, :C_CHUNK].T, C_GDIM, axis=1)
    v_p, z_p = gmlp_mix(uv, g_v, w_s[:, :C_CHUNK, :C_CHUNK], bias_full, Mp)
    w0 = jnp.repeat(w_s[:, 0, 0], C_GDIM).reshape(1, W)
    b0 = jnp.repeat(b_s[:, 0], C_GDIM).reshape(1, W)
    v_s, z_s = gmlp_single(uv[Mp:], g_v, w0, b0)
    z = jnp.concatenate([z_p, z_s], axis=0)
    x = mm(z, w_o_all, F32, "gmlp_out", res=x, scale=1.0, layer=j, tn=512)
    last = ((seq - 1) // C_CHUNK) * C_CHUNK
    v_last = v_p.reshape(n_batch, seq, W)[:, last:]
    return x, v_last, v_s


def _forward(x_prompt, x_sample, cache_a_ckv, cache_a_kr, state_b, page_table, norm_ffn_a,
             ffn_a_wi, ffn_a_wo, norm_mix, a_w_in, a_g_q, a_g_kv, a_w_uq, a_w_ukv, a_w_o,
             b_w_in, b_lower_bounds, b_g_o, b_w_o, c_w_in, c_g_v, c_w_s, c_b_s, c_w_o,
             norm_ffn_b, ffn_b_wi, ffn_b_wo, final_norm, *, flash_tile, scan_tb):
    n_batch, seq, D = x_prompt.shape
    Bd, dec_seq, _ = x_sample.shape
    assert dec_seq == 1
    depth = norm_mix.shape[0]
    Mp = n_batch * seq
    past_len = page_table.shape[1] * PAGE_SIZE
    x = jnp.concatenate([x_prompt.reshape(Mp, D), x_sample.reshape(Bd, D)], axis=0)

    pos = jnp.concatenate([jnp.tile(jnp.arange(seq), n_batch),
                           jnp.full((Bd,), past_len, jnp.int32)])
    cos_t, sin_t = _rope_tables(pos)

    cache_krt = jnp.swapaxes(cache_a_kr, 2, 3)

    a_ckv, a_kr, b_sp, b_ss, c_vp, c_vs = [], [], [], [], [], []
    for i in range(depth):
        x = ffn_half_step(x, norm_ffn_a[i], ffn_a_wi, ffn_a_wo, i)
        j = i // N_MIXERS
        kind = i % N_MIXERS
        if kind == 0:
            x, ckv, kr = mla_layer(x, norm_mix[i], a_w_in[j], a_g_q[j], a_g_kv[j], a_w_uq[j],
                                   a_w_ukv[j], a_w_o, cache_a_ckv, cache_krt, page_table,
                                   j, n_batch, seq, cos_t, sin_t, flash_tile)
            a_ckv.append(ckv)
            a_kr.append(kr)
        elif kind == 1:
            x, sp, ss = hgrn_layer(x, norm_mix[i], b_w_in, b_lower_bounds, b_g_o[j], b_w_o,
                                   state_b[j], j, i, n_batch, seq, scan_tb)
            b_sp.append(sp)
            b_ss.append(ss)
        else:
            x, vp, vs = gmlp_layer(x, norm_mix[i], c_w_in, c_g_v[j], c_w_s[j], c_b_s[j],
                                   c_w_o, j, n_batch, seq)
            c_vp.append(vp)
            c_vs.append(vs)
        x = ffn_half_step(x, norm_ffn_b[i], ffn_b_wi, ffn_b_wo, i)

    y_p = final_norm_call(x, final_norm, 0, Mp, min(1024, Mp))
    y_s = final_norm_call(x, final_norm, Mp, Bd, Bd)
    ckv_all = jnp.stack(a_ckv)
    kr_all = jnp.stack(a_kr)
    return (y_p.reshape(n_batch, seq, D), y_s.reshape(Bd, 1, D),
            ckv_all[:, :Mp].reshape(-1, n_batch, seq, A_KV_LORA),
            kr_all[:, :Mp].reshape(-1, n_batch, seq, A_ROPE),
            ckv_all[:, Mp:].reshape(-1, Bd, 1, A_KV_LORA),
            kr_all[:, Mp:].reshape(-1, Bd, 1, A_ROPE),
            jnp.stack(b_sp), jnp.stack(b_ss),
            jnp.stack(c_vp), jnp.stack(c_vs)[:, :, None, :])


def kernel(x_prompt, x_sample, cache_a_ckv, cache_a_kr, state_b, page_table, norm_ffn_a, ffn_a_wi, ffn_a_wo, norm_mix, a_w_in, a_g_q, a_g_kv, a_w_uq, a_w_ukv, a_w_o, b_w_in, b_lower_bounds, b_g_o, b_w_o, c_w_in, c_g_v, c_w_s, c_b_s, c_w_o, norm_ffn_b, ffn_b_wi, ffn_b_wo, final_norm):
    seq = x_prompt.shape[1]
    return _forward(x_prompt, x_sample, cache_a_ckv, cache_a_kr, state_b, page_table, norm_ffn_a,
                    ffn_a_wi, ffn_a_wo, norm_mix, a_w_in, a_g_q, a_g_kv, a_w_uq, a_w_ukv, a_w_o,
                    b_w_in, b_lower_bounds, b_g_o, b_w_o, c_w_in, c_g_v, c_w_s, c_b_s, c_w_o,
                    norm_ffn_b, ffn_b_wi, ffn_b_wo, final_norm,
                    flash_tile=min(512, seq // 2), scan_tb=min(256, seq))
```

```python
import functools
import math

import jax
import jax.numpy as jnp
from jax import lax
from jax.experimental import pallas as pl
from jax.experimental.pallas import tpu as pltpu

F32 = jnp.float32
BF16 = jnp.bfloat16

EPS = 1e-6
ROPE_THETA = 10000.0
N_MIXERS = 3

A_HEADS = 16
A_NOPE = 128
A_ROPE = 64
A_VDIM = 128
A_Q_LORA = 512
A_KV_LORA = 512
A_QK_PAD = 256
Q_PRESCALE = (A_NOPE + A_ROPE) ** -0.5 * math.log2(math.e)
PAGE_SIZE = 128
N_DECODE_SLOTS = 4

B_HEADS = 16
B_DK = 128
B_DV = 128

C_GROUPS = 16
C_GDIM = 128
C_CHUNK = 128

MIB = 1024 * 1024


def _cparams(semantics, vmem_mib=48):
    return pltpu.CompilerParams(dimension_semantics=semantics,
                                vmem_limit_bytes=vmem_mib * MIB)


def _rms(x, g):
    return x * lax.rsqrt(jnp.mean(x * x, axis=-1, keepdims=True) + EPS) * g


def _sigmoid(x):
    return 1.0 / (1.0 + jnp.exp(-x))


def _silu(x):
    return x * _sigmoid(x)


def _gelu_tanh(x):
    c = math.sqrt(2.0 / math.pi)
    return 0.5 * x * (1.0 + jnp.tanh(c * (x + 0.044715 * (x * x * x))))


def _row_tile(m, cap):
    best = None
    for t in range(16, cap + 1, 16):
        if m % t == 0:
            best = t
    assert best is not None, (m, cap)
    return best


def _norm_mm_kernel(x_ref, g_ref, *refs, n_w, epilogue):
    w_refs = refs[:n_w]
    o_ref = refs[n_w]
    h_ref = refs[n_w + 1]

    @pl.when(pl.program_id(1) == 0)
    def _():
        h_ref[...] = _rms(x_ref[...], g_ref[...]).astype(BF16)

    h = h_ref[...]
    accs = [jnp.dot(h, w[...].astype(BF16), preferred_element_type=F32) for w in w_refs]
    o_ref[...] = epilogue(*accs).astype(o_ref.dtype)


def _weight_spec(w, layer, rows, tn, col_block):
    if layer is None:
        assert w.ndim == 2
        return pl.BlockSpec((rows, tn), lambda i, j: (0, col_block(i, j)))
    assert w.ndim == 3
    return pl.BlockSpec((None, rows, tn), lambda i, j: (layer, 0, col_block(i, j)))


def norm_mm(x, g, w, col_offsets, out_cols, tn, epilogue, out_dtype, name, layer=None, tm_cap=1040):
    M, D = x.shape
    tm = _row_tile(M, tm_cap)
    n_w = len(col_offsets)
    in_specs = [pl.BlockSpec((tm, D), lambda i, j: (i, 0)),
                pl.BlockSpec((1, D), lambda i, j: (0, 0))]
    for off in col_offsets:
        assert off % tn == 0
        in_specs.append(_weight_spec(w, layer, D, tn, lambda i, j, o=off // tn: j + o))
    return pl.pallas_call(
        functools.partial(_norm_mm_kernel, n_w=n_w, epilogue=epilogue),
        grid=(M // tm, out_cols // tn),
        in_specs=in_specs,
        out_specs=pl.BlockSpec((tm, tn), lambda i, j: (i, j)),
        out_shape=jax.ShapeDtypeStruct((M, out_cols), out_dtype),
        scratch_shapes=[pltpu.VMEM((tm, D), BF16)],
        compiler_params=_cparams(("parallel", "arbitrary")),
        name=name,
    )(x, g.reshape(1, D), *([w] * n_w))


def _mm_kernel(a_ref, w_ref, *refs, scale, has_res):
    o_ref = refs[-1]
    acc = jnp.dot(a_ref[...], w_ref[...].astype(BF16), preferred_element_type=F32)
    if has_res:
        acc = refs[0][...] + scale * acc
    o_ref[...] = acc.astype(o_ref.dtype)


def mm(a, w, out_dtype, name, res=None, scale=1.0, layer=None, tm_cap=1040, tn=512, vmem_mib=48):
    M, K = a.shape
    N = w.shape[-1]
    tm = _row_tile(M, tm_cap)
    tn = min(tn, N)
    in_specs = [pl.BlockSpec((tm, K), lambda i, j: (i, 0)),
                _weight_spec(w, layer, K, tn, lambda i, j: j)]
    args = [a, w]
    if res is not None:
        in_specs.append(pl.BlockSpec((tm, tn), lambda i, j: (i, j)))
        args.append(res)
    return pl.pallas_call(
        functools.partial(_mm_kernel, scale=scale, has_res=res is not None),
        grid=(M // tm, N // tn),
        in_specs=in_specs,
        out_specs=pl.BlockSpec((tm, tn), lambda i, j: (i, j)),
        out_shape=jax.ShapeDtypeStruct((M, N), out_dtype),
        compiler_params=_cparams(("parallel", "parallel"), vmem_mib),
        name=name,
    )(*args)


def _head_mm_kernel(x_ref, w_ref, o_ref, *, heads, k, n):
    for h in range(heads):
        o_ref[:, h * n:(h + 1) * n] = jnp.dot(
            x_ref[:, h * k:(h + 1) * k], w_ref[h], preferred_element_type=F32).astype(o_ref.dtype)


def head_mm(x, w, out_dtype, name):
    R = x.shape[0]
    H, K, N = w.shape
    return pl.pallas_call(
        functools.partial(_head_mm_kernel, heads=H, k=K, n=N),
        grid=(1,),
        in_specs=[pl.BlockSpec((R, H * K), lambda i: (0, 0)),
                  pl.BlockSpec((H, K, N), lambda i: (0, 0, 0))],
        out_specs=pl.BlockSpec((R, H * N), lambda i: (0, 0)),
        out_shape=jax.ShapeDtypeStruct((R, H * N), out_dtype),
        compiler_params=_cparams(("arbitrary",)),
        name=name,
    )(x, w)


def _norm_kernel(x_ref, g_ref, o_ref):
    o_ref[...] = _rms(x_ref[...], g_ref[...])


def final_norm_call(x, g, row0, n_rows, tm):
    D = x.shape[1]
    assert row0 % tm == 0 and n_rows % tm == 0
    blk0 = row0 // tm
    return pl.pallas_call(
        _norm_kernel,
        grid=(n_rows // tm,),
        in_specs=[pl.BlockSpec((tm, D), lambda i: (i + blk0, 0)),
                  pl.BlockSpec((1, D), lambda i: (0, 0))],
        out_specs=pl.BlockSpec((tm, D), lambda i: (i, 0)),
        out_shape=jax.ShapeDtypeStruct((n_rows, D), F32),
        compiler_params=_cparams(("parallel",)),
        name="final_norm",
    )(x, g.reshape(1, D))


def _swiglu_epilogue(gate, up):
    return _silu(gate) * up


def ffn_half_step(x, g, wi, wo, layer):
    d_ff = wo.shape[1]
    hid = norm_mm(x, g, wi, (0, d_ff), d_ff, 512, _swiglu_epilogue, BF16, "ffn_in", layer=layer)
    return mm(hid, wo, F32, "ffn_out", res=x, scale=0.5, layer=layer, tm_cap=832, tn=512, vmem_mib=56)


def _mla_in_kernel(x_ref, g_ref, w_ref, gq_ref, gkv_ref, cos_ref, sin_ref,
                   cq_ref, ckv_ref, ckvb_ref, kr_ref, krb_ref):
    h = _rms(x_ref[...], g_ref[...]).astype(BF16)
    acc = jnp.dot(h, w_ref[...], preferred_element_type=F32)
    ql, kl = A_Q_LORA, A_KV_LORA
    cq_ref[...] = _rms(acc[:, :ql], gq_ref[...]).astype(BF16)
    ckv = _rms(acc[:, ql:ql + kl], gkv_ref[...])
    ckv_ref[...] = ckv
    ckvb_ref[...] = ckv.astype(BF16)
    kr = (acc[:, ql + kl:ql + kl + 128] * cos_ref[...]
          + acc[:, ql + kl + 128:ql + kl + 256] * sin_ref[...])
    kr_ref[...] = kr
    krb_ref[...] = kr.astype(BF16)


def mla_in(x, g, w_ext, g_q, g_kv, cos_t, sin_t):
    M, D = x.shape
    tm = _row_tile(M, 640)
    NW = w_ext.shape[1]
    row = lambda n: pl.BlockSpec((tm, n), lambda i: (i, 0))
    full = lambda a, b: pl.BlockSpec((a, b), lambda i: (0, 0))
    return pl.pallas_call(
        _mla_in_kernel,
        grid=(M // tm,),
        in_specs=[row(D), full(1, D), full(D, NW), full(1, A_Q_LORA), full(1, A_KV_LORA),
                  row(128), row(128)],
        out_specs=[row(A_Q_LORA), row(A_KV_LORA), row(A_KV_LORA), row(128), row(128)],
        out_shape=[jax.ShapeDtypeStruct((M, A_Q_LORA), BF16),
                   jax.ShapeDtypeStruct((M, A_KV_LORA), F32),
                   jax.ShapeDtypeStruct((M, A_KV_LORA), BF16),
                   jax.ShapeDtypeStruct((M, 128), F32),
                   jax.ShapeDtypeStruct((M, 128), BF16)],
        compiler_params=_cparams(("parallel",)),
        name="mla_in",
    )(x, g.reshape(1, D), w_ext, g_q.reshape(1, -1), g_kv.reshape(1, -1), cos_t, sin_t)


def _mla_q_kernel(cq_ref, w1_ref, w2_ref, cos_ref, sin_ref, o_ref):
    cq = cq_ref[...]
    cos = cos_ref[...]
    sin = sin_ref[...]
    P = A_QK_PAD
    for h in range(A_HEADS):
        a1 = jnp.dot(cq, w1_ref[:, h * P:(h + 1) * P], preferred_element_type=F32)
        a2 = jnp.dot(cq, w2_ref[:, h * 128:(h + 1) * 128], preferred_element_type=F32)
        o_ref[:, h * P:h * P + 128] = (a1[:, :128] * Q_PRESCALE).astype(BF16)
        o_ref[:, h * P + 128:(h + 1) * P] = ((a1[:, 128:] * cos + a2 * sin) * Q_PRESCALE).astype(BF16)


def mla_q(cq, w1, w2, cos_t, sin_t):
    M = cq.shape[0]
    tm = _row_tile(M, 640)
    row = lambda n: pl.BlockSpec((tm, n), lambda i: (i, 0))
    full = lambda a, b: pl.BlockSpec((a, b), lambda i: (0, 0))
    return pl.pallas_call(
        _mla_q_kernel,
        grid=(M // tm,),
        in_specs=[row(A_Q_LORA), full(*w1.shape), full(*w2.shape), row(128), row(128)],
        out_specs=row(A_HEADS * A_QK_PAD),
        out_shape=jax.ShapeDtypeStruct((M, A_HEADS * A_QK_PAD), BF16),
        compiler_params=_cparams(("parallel",)),
        name="mla_q",
    )(cq, w1, w2, cos_t, sin_t)


def _mla_kv_kernel(c_ref, wk_ref, wvt_ref, kn_ref, vt_ref):
    c = c_ref[...]
    kn_ref[...] = jnp.dot(c, wk_ref[...], preferred_element_type=F32).astype(BF16)
    vt_ref[...] = lax.dot_general(wvt_ref[...], c, (((1,), (1,)), ((), ())),
                                  preferred_element_type=F32).astype(BF16)


def mla_kv(ckvb, w_uk_cols, w_uv_rows, n_rows, tm):
    C = A_KV_LORA
    N = w_uk_cols.shape[1]
    NV = w_uv_rows.shape[0]
    return pl.pallas_call(
        _mla_kv_kernel,
        grid=(n_rows // tm,),
        in_specs=[pl.BlockSpec((tm, C), lambda i: (i, 0)),
                  pl.BlockSpec((C, N), lambda i: (0, 0)),
                  pl.BlockSpec((N, C), lambda i: (0, 0))],
        out_specs=[pl.BlockSpec((tm, N), lambda i: (i, 0)),
                   pl.BlockSpec((NV, tm), lambda i: (0, i))],
        out_shape=[jax.ShapeDtypeStruct((n_rows, N), BF16),
                   jax.ShapeDtypeStruct((NV, n_rows), BF16)],
        compiler_params=_cparams(("parallel",)),
        name="mla_kv",
    )(ckvb, w_uk_cols, w_uv_rows)


def _flash_kernel(q_ref, kn_ref, kr_ref, vt_ref, o_ref, m_ref, l_ref, acc_ref, *, half, n_pairs):
    nt_dims = (((1,), (1,)), ((), ()))

    def update(j, q0, k0, klen, masked):
        q = q_ref[pl.ds(q0, half), :]
        k = jnp.concatenate([kn_ref[pl.ds(k0, klen), :], kr_ref[pl.ds(k0, klen), :]], axis=1)
        st = lax.dot_general(k, q, nt_dims, preferred_element_type=F32)
        if masked:
            kpos = lax.broadcasted_iota(jnp.int32, st.shape, 0)
            qpos = lax.broadcasted_iota(jnp.int32, st.shape, 1)
            st = jnp.where(kpos <= qpos, st, -jnp.inf)
        m_prev = m_ref[j]
        m_new = jnp.maximum(m_prev, jnp.max(st, axis=0, keepdims=True))
        alpha = jnp.exp2(m_prev - m_new)
        p = jnp.exp2(st - m_new)
        l_ref[j] = alpha * l_ref[j] + jnp.sum(p, axis=0, keepdims=True)
        acc_ref[j] = alpha * acc_ref[j] + jnp.dot(vt_ref[:, pl.ds(k0, klen)], p.astype(BF16),
                                                  preferred_element_type=F32)
        m_ref[j] = m_new

    def finish(j):
        return (acc_ref[j] / l_ref[j]).T.astype(o_ref.dtype)

    def pair(p, carry):
        qa = pl.multiple_of(p * (2 * half), 2 * half)
        qb = pl.multiple_of(qa + half, half)
        m_ref[...] = jnp.full(m_ref.shape, -jnp.inf, F32)
        l_ref[...] = jnp.zeros(l_ref.shape, F32)
        acc_ref[...] = jnp.zeros(acc_ref.shape, F32)

        def k_step(kc, c):
            k0 = pl.multiple_of(kc * (2 * half), 2 * half)
            update(0, qa, k0, 2 * half, False)
            update(1, qb, k0, 2 * half, False)
            return c

        lax.fori_loop(0, p, k_step, 0)
        update(0, qa, qa, half, True)
        update(1, qb, qa, half, False)
        update(1, qb, qb, half, True)
        o_ref[pl.ds(qa, half), :] = finish(0)
        o_ref[pl.ds(qb, half), :] = finish(1)
        return carry

    lax.fori_loop(0, n_pairs, pair, 0)


def mla_flash(qp, kn, krb, vt, n_batch, seq, half):
    H = A_HEADS
    assert seq % (2 * half) == 0
    return pl.pallas_call(
        functools.partial(_flash_kernel, half=half, n_pairs=seq // (2 * half)),
        grid=(n_batch, H),
        in_specs=[pl.BlockSpec((seq, A_QK_PAD), lambda b, h: (b, h)),
                  pl.BlockSpec((seq, A_NOPE), lambda b, h: (b, h)),
                  pl.BlockSpec((seq, 128), lambda b, h: (b, 0)),
                  pl.BlockSpec((A_VDIM, seq), lambda b, h: (h, b))],
        out_specs=pl.BlockSpec((seq, A_VDIM), lambda b, h: (b, h)),
        out_shape=jax.ShapeDtypeStruct((n_batch * seq, H * A_VDIM), BF16),
        scratch_shapes=[pltpu.VMEM((2, 1, half), F32), pltpu.VMEM((2, 1, half), F32),
                        pltpu.VMEM((2, A_VDIM, half), F32)],
        compiler_params=_cparams(("parallel", "parallel")),
        name="mla_flash",
    )(qp, kn, krb, vt)


def _decode_kernel(pt_ref, ql_ref, qr_ref, cn_ref, rn_ref, ckv_hbm, krt_hbm, o_ref,
                   cbuf, rbuf, sems, m_ref, l_ref, acc_ref,
                   *, layer, n_pages, chunk, group):
    g = pl.program_id(0)
    n_chunks = n_pages // chunk
    last = 2 * pl.num_programs(0) - 1
    c0 = (2 * g) % n_chunks

    def copies(k, slot):
        out = []
        rg = k // n_chunks
        cc = k % n_chunks
        for r in range(group):
            base = (rg * group + r) * n_pages + cc * chunk
            for p in range(chunk):
                page = pt_ref[base + p]
                out.append(pltpu.make_async_copy(ckv_hbm.at[layer, page], cbuf.at[slot, r, p],
                                                 sems.at[0, slot]))
                out.append(pltpu.make_async_copy(krt_hbm.at[layer, page], rbuf.at[slot, r, p],
                                                 sems.at[1, slot]))
        return out

    def start(k, slot):
        for cp in copies(k, slot):
            cp.start()

    def wait(k, slot):
        for cp in copies(k, slot):
            cp.wait()

    s0 = (2 * g) % N_DECODE_SLOTS

    @pl.when(g == 0)
    def _():
        start(0, 0)
        start(1, 1)

    @pl.when(c0 == 0)
    def _():
        for r in range(group):
            cn = cn_ref[r].astype(F32)
            rn = rn_ref[r].astype(F32)
            s_new = (jnp.sum(ql_ref[r].astype(F32) * cn, axis=-1, keepdims=True)
                     + jnp.sum(qr_ref[r].astype(F32) * rn, axis=-1, keepdims=True))
            m_ref[r] = s_new
            l_ref[r] = jnp.ones(s_new.shape, F32)
            acc_ref[r] = jnp.broadcast_to(cn, acc_ref.shape[1:])

    nt_dims = (((1,), (1,)), ((), ()))

    def attend(slot):
        cbs, scores = [], []
        for r in range(group):
            cb = cbuf[slot, r].reshape(chunk * PAGE_SIZE, A_KV_LORA).astype(BF16)
            s_rope = jnp.concatenate(
                [jnp.dot(qr_ref[r], rbuf[slot, r, p].astype(BF16), preferred_element_type=F32)
                 for p in range(chunk)], axis=1)
            cbs.append(cb)
            scores.append(lax.dot_general(ql_ref[r], cb, nt_dims, preferred_element_type=F32) + s_rope)
        for r in range(group):
            s = scores[r]
            m_prev = m_ref[r]
            m_new = jnp.maximum(m_prev, jnp.max(s, axis=-1, keepdims=True))
            alpha = jnp.exp2(m_prev - m_new)
            p = jnp.exp2(s - m_new)
            l_ref[r] = alpha * l_ref[r] + jnp.sum(p, axis=-1, keepdims=True)
            acc_ref[r] = alpha * acc_ref[r] + jnp.dot(p.astype(BF16), cbs[r], preferred_element_type=F32)
            m_ref[r] = m_new

    start(jnp.minimum(2 * g + 2, last), (s0 + 2) % N_DECODE_SLOTS)
    wait(2 * g, s0)
    attend(s0)
    start(jnp.minimum(2 * g + 3, last), (s0 + 3) % N_DECODE_SLOTS)
    wait(2 * g + 1, s0 + 1)
    attend(s0 + 1)

    @pl.when(c0 + 1 == n_chunks - 1)
    def _():
        for r in range(group):
            o_ref[r] = (acc_ref[r] / l_ref[r]).astype(o_ref.dtype)

    @pl.when(2 * g + 1 == last)
    def _():
        wait(last, (s0 + 2) % N_DECODE_SLOTS)
        wait(last, (s0 + 3) % N_DECODE_SLOTS)


def mla_decode(page_table, ql, qr, cn, rn, cache_ckv, cache_krt, layer, chunk=8, group=2):
    Bd, n_pages = page_table.shape
    assert n_pages % (2 * chunk) == 0 and Bd % group == 0
    steps_per_group = n_pages // (2 * chunk)
    H, C, R = A_HEADS, A_KV_LORA, A_ROPE
    req = lambda shp: pl.BlockSpec((group,) + shp, lambda g, pt: (g // steps_per_group, 0, 0))
    grid_spec = pltpu.PrefetchScalarGridSpec(
        num_scalar_prefetch=1,
        grid=(Bd // group * steps_per_group,),
        in_specs=[req((H, C)), req((H, R)), req((1, C)), req((1, R)),
                  pl.BlockSpec(memory_space=pl.ANY), pl.BlockSpec(memory_space=pl.ANY)],
        out_specs=req((H, C)),
        scratch_shapes=[pltpu.VMEM((N_DECODE_SLOTS, group, chunk, PAGE_SIZE, C), F32),
                        pltpu.VMEM((N_DECODE_SLOTS, group, chunk, R, PAGE_SIZE), F32),
                        pltpu.SemaphoreType.DMA((2, N_DECODE_SLOTS)),
                        pltpu.VMEM((group, H, 1), F32), pltpu.VMEM((group, H, 1), F32),
                        pltpu.VMEM((group, H, C), F32)])
    return pl.pallas_call(
        functools.partial(_decode_kernel, layer=layer, n_pages=n_pages, chunk=chunk, group=group),
        grid_spec=grid_spec,
        out_shape=jax.ShapeDtypeStruct((Bd, H, C), BF16),
        compiler_params=_cparams(("arbitrary",)),
        name="mla_decode",
    )(page_table.reshape(-1), ql, qr, cn, rn, cache_ckv, cache_krt)


def _rope_tables(pos):
    half = A_ROPE // 2
    inv = ROPE_THETA ** (-jnp.arange(half, dtype=F32) / half)
    ang = pos.astype(F32)[:, None] * inv[None, :]
    cos, sin = jnp.cos(ang), jnp.sin(ang)
    z = jnp.zeros((pos.shape[0], 128 - A_ROPE), F32)
    return (jnp.concatenate([cos, cos, z], axis=1), jnp.concatenate([sin, sin, z], axis=1))


def _half_swap(w):
    half = A_ROPE // 2
    return jnp.concatenate([-w[..., half:], w[..., :half]], axis=-1)


def mla_layer(x, g_mix, w_in, g_q, g_kv, w_uq, w_ukv, w_o_all, cache_ckv, cache_krt, page_table,
              layer, n_batch, seq, cos_t, sin_t, flash_tile):
    D = x.shape[1]
    Mp = n_batch * seq
    H = A_HEADS
    lo = A_Q_LORA + A_KV_LORA
    zpad = jnp.zeros((D, 128 - A_ROPE), F32)
    w_kr = w_in[:, lo:]
    w_in_ext = jnp.concatenate([w_in[:, :lo], w_kr, zpad, _half_swap(w_kr), zpad], axis=1).astype(BF16)
    cq, ckv, ckvb, kr, krb = mla_in(x, g_mix, w_in_ext, g_q, g_kv, cos_t, sin_t)

    wq = w_uq.reshape(A_Q_LORA, H, A_NOPE + A_ROPE)
    wq_rope = wq[..., A_NOPE:]
    zq = jnp.zeros((A_Q_LORA, H, 128 - A_ROPE), F32)
    w1 = jnp.concatenate([wq[..., :A_NOPE], wq_rope, zq], axis=-1).reshape(A_Q_LORA, H * A_QK_PAD)
    w2 = jnp.concatenate([_half_swap(wq_rope), zq], axis=-1).reshape(A_Q_LORA, H * 128)
    qp = mla_q(cq, w1.astype(BF16), w2.astype(BF16), cos_t, sin_t)

    wkv = w_ukv.reshape(A_KV_LORA, H, A_NOPE + A_VDIM)
    w_uk = wkv[..., :A_NOPE]
    w_uv = wkv[..., A_NOPE:]
    kn, vt = mla_kv(ckvb, w_uk.reshape(A_KV_LORA, H * A_NOPE).astype(BF16),
                    jnp.transpose(w_uv, (1, 2, 0)).reshape(H * A_VDIM, A_KV_LORA).astype(BF16),
                    Mp, min(1024, seq))
    o_p = mla_flash(qp, kn, krb, vt, n_batch, seq, flash_tile)

    Bd = page_table.shape[0]
    qs = qp[Mp:].reshape(Bd, H, A_QK_PAD)
    q_nope = qs[:, :, :A_NOPE].reshape(Bd, H * A_NOPE)
    q_rope = qs[:, :, A_NOPE:A_NOPE + A_ROPE]
    q_lat = head_mm(q_nope, jnp.transpose(w_uk, (1, 2, 0)).astype(BF16), BF16, "mla_qlat")
    o_lat = mla_decode(page_table, q_lat.reshape(Bd, H, A_KV_LORA), q_rope,
                       ckvb[Mp:].reshape(Bd, 1, A_KV_LORA),
                       krb[Mp:, :A_ROPE].reshape(Bd, 1, A_ROPE),
                       cache_ckv, cache_krt, layer)
    o_s = head_mm(o_lat.reshape(Bd, H * A_KV_LORA), jnp.transpose(w_uv, (1, 0, 2)).astype(BF16), BF16,
                  "mla_ouv")

    o = jnp.concatenate([o_p, o_s], axis=0)
    x = mm(o, w_o_all, F32, "mla_out", res=x, scale=1.0, layer=layer, tn=1024)
    return x, ckv, kr[:, :A_ROPE]


def _lower_bound(raw, layer):
    e = jnp.exp(raw - jnp.max(raw, axis=0, keepdims=True))
    tot = jnp.sum(e, axis=0, keepdims=True)
    if layer == 0:
        return jnp.zeros_like(tot)
    return jnp.sum(e[1:layer + 1], axis=0, keepdims=True) / tot


def _hgrn_scan_kernel(q_ref, f_ref, i_ref, og_ref, lbr_ref, go_ref,
                      seg_ref, pm_ref, rm_ref, o_ref, st_ref, state_ref, a_ref, *, layer, tb, hb):
    t = pl.program_id(2)
    n_lev = pm_ref.shape[0]
    tn_dims = (((0,), (0,)), ((), ()))
    nt_dims = (((1,), (1,)), ((), ()))

    @pl.when(t == 0)
    def _():
        state_ref[...] = jnp.zeros(state_ref.shape, F32)

    lb = _lower_bound(lbr_ref[...], layer)
    gate = lb + (1.0 - lb) * _sigmoid(f_ref[...])
    logf = jnp.log(gate)
    q = _silu(q_ref[...])
    k = 1.0 - gate
    vb = i_ref[...].astype(BF16)
    a1 = logf.astype(BF16)
    r1 = logf - a1.astype(F32)
    a2 = r1.astype(BF16)
    a3 = (r1 - a2.astype(F32)).astype(BF16)

    def seg_sum(lev):
        m = seg_ref[lev]
        return (jnp.dot(m, a1, preferred_element_type=F32) + jnp.dot(m, a2, preferred_element_type=F32)
                + jnp.dot(m, a3, preferred_element_type=F32))

    b = seg_sum(n_lev)
    width = b.shape[1]

    for lev in range(n_lev):
        right = jnp.concatenate([rm_ref[lev]] * hb, axis=1)
        blk = tb >> lev
        if blk >= 8:
            b3 = b.reshape(tb // blk, blk, width)
            b_split = jnp.broadcast_to(b3[:, blk // 2 - 1:blk // 2, :], b3.shape).reshape(tb, width)
            x = -jnp.abs(b - b_split)
        else:
            x = seg_sum(lev)
        e = jnp.exp(x)
        qh = (q * e * right).astype(BF16)
        kh = (k * e * (1.0 - right)).astype(BF16)
        for hh in range(hb):
            ls = slice(hh * B_DK, (hh + 1) * B_DK)
            s = lax.dot_general(qh[:, ls], kh[:, ls], nt_dims, preferred_element_type=F32) * pm_ref[lev]
            if lev == 0:
                a_ref[hh] = s
            else:
                a_ref[hh] = a_ref[hh] + s

    b_last = b[tb - 1:tb, :]
    qe = (q * jnp.exp(b)).astype(BF16)
    kd = (k * jnp.exp(b_last - b)).astype(BF16)
    decay = jnp.exp(b_last)
    qk = q * k
    for hh in range(hb):
        ls = slice(hh * B_DK, (hh + 1) * B_DK)
        st = state_ref[hh]
        v = vb[:, ls]
        o = (lax.dot_general(qe[:, ls], st.astype(BF16), nt_dims, preferred_element_type=F32)
             + jnp.dot(a_ref[hh].astype(BF16), v, preferred_element_type=F32)
             + jnp.sum(qk[:, ls], axis=-1, keepdims=True) * i_ref[:, ls])
        state_ref[hh] = st * decay[:, ls] + lax.dot_general(v, kd[:, ls], tn_dims,
                                                            preferred_element_type=F32)
        o = _rms(o, go_ref[:, ls]) * _silu(og_ref[:, ls])
        o_ref[:, ls] = o.astype(o_ref.dtype)

    @pl.when(t == pl.num_programs(2) - 1)
    def _():
        for hh in range(hb):
            st_ref[0, hh] = state_ref[hh].T


def _scan_constants(tb):
    assert tb & (tb - 1) == 0
    t = jnp.arange(tb)[:, None]
    j = jnp.arange(tb)[None, :]
    segs, pms, rms = [], [], []
    h = tb // 2
    while h >= 1:
        split = (t // (2 * h)) * (2 * h) + h - 1
        right = t > split
        segs.append(jnp.where(right, (j > split) & (j <= t), (j > t) & (j <= split)))
        pms.append((t // (2 * h)) == (j // (2 * h)))
        rms.append(jnp.broadcast_to(right, (tb, B_DK)))
        h //= 2
    segs.append(j <= t)
    return (jnp.stack(segs).astype(BF16), jnp.stack(pms).astype(F32), jnp.stack(rms).astype(F32))


def hgrn_scan(proj, lb_raw, g_o, layer, n_batch, seq, tb, hb=8):
    H = B_HEADS
    assert H % hb == 0
    nt = seq // tb
    W = hb * B_DK
    seg, pm, rm = _scan_constants(tb)
    sec = lambda k: pl.BlockSpec((tb, W), lambda b, h, t, k=k: (b * nt + t, k * (H // hb) + h))
    const = lambda a: pl.BlockSpec(a.shape, lambda b, h, t: (0, 0, 0))
    return pl.pallas_call(
        functools.partial(_hgrn_scan_kernel, layer=layer, tb=tb, hb=hb),
        grid=(n_batch, H // hb, nt),
        in_specs=[sec(0), sec(1), sec(2), sec(3),
                  pl.BlockSpec((lb_raw.shape[0], W), lambda b, h, t: (0, h)),
                  pl.BlockSpec((1, W), lambda b, h, t: (0, h)),
                  const(seg), const(pm), const(rm)],
        out_specs=[pl.BlockSpec((tb, W), lambda b, h, t: (b * nt + t, h)),
                   pl.BlockSpec((1, hb, B_DK, B_DV), lambda b, h, t: (b, h, 0, 0))],
        out_shape=[jax.ShapeDtypeStruct((n_batch * seq, H * B_DV), BF16),
                   jax.ShapeDtypeStruct((n_batch, H, B_DK, B_DV), F32)],
        scratch_shapes=[pltpu.VMEM((hb, B_DV, B_DK), F32), pltpu.VMEM((hb, tb, tb), F32)],
        compiler_params=_cparams(("parallel", "parallel", "arbitrary")),
        name="hgrn_scan",
    )(proj, proj, proj, proj, lb_raw, g_o.reshape(1, -1), seg, pm, rm)


def _hgrn_step_kernel(p_ref, s_ref, lbr_ref, go_ref, o_ref, so_ref, *, layer):
    H = B_HEADS
    p = p_ref[0]
    lb = _lower_bound(lbr_ref[...], layer)[0]
    gate = lb + (1.0 - lb) * _sigmoid(p[H:2 * H])
    q = _silu(p[0:H])
    v = p[2 * H:3 * H]
    og = p[3 * H:4 * H]
    packed = jnp.concatenate([gate, jnp.zeros((128 - H, B_DK), F32)], axis=0)
    cols = packed.T
    qb = q.astype(BF16)
    outs = []
    for h in range(H):
        g_col = cols[:, h:h + 1]
        s_new = g_col * s_ref[0, h] + (1.0 - g_col) * v[h:h + 1, :]
        so_ref[0, h] = s_new
        outs.append(jnp.dot(qb, s_new.astype(BF16), preferred_element_type=F32)[h:h + 1, :])
    o = jnp.concatenate(outs, axis=0)
    o_ref[0] = (_rms(o, go_ref[...]) * _silu(og)).astype(o_ref.dtype)


def hgrn_step(proj_s, state, lb_raw, g_o, layer):
    Bd = proj_s.shape[0]
    H = B_HEADS
    depth = lb_raw.shape[0]
    o, s_new = pl.pallas_call(
        functools.partial(_hgrn_step_kernel, layer=layer),
        grid=(Bd,),
        in_specs=[pl.BlockSpec((1, 4 * H, 128), lambda r: (r, 0, 0)),
                  pl.BlockSpec((1, H, B_DK, B_DV), lambda r: (r, 0, 0, 0)),
                  pl.BlockSpec((depth, H, B_DK), lambda r: (0, 0, 0)),
                  pl.BlockSpec((H, B_DV), lambda r: (0, 0))],
        out_specs=[pl.BlockSpec((1, H, B_DV), lambda r: (r, 0, 0)),
                   pl.BlockSpec((1, H, B_DK, B_DV), lambda r: (r, 0, 0, 0))],
        out_shape=[jax.ShapeDtypeStruct((Bd, H, B_DV), BF16),
                   jax.ShapeDtypeStruct(state.shape, F32)],
        compiler_params=_cparams(("parallel",)),
        name="hgrn_step",
    )(proj_s.reshape(Bd, 4 * H, 128), state, lb_raw.reshape(depth, H, B_DK), g_o.reshape(H, B_DV))
    return o.reshape(Bd, H * B_DV), s_new


def hgrn_layer(x, g_mix, w_in_all, lb_raw, g_o, w_o_all, state, j, layer, n_batch, seq, scan_tb):
    Mp = n_batch * seq
    proj = norm_mm(x, g_mix, w_in_all, (0,), w_in_all.shape[-1], 1024, lambda a: a, F32, "hgrn_in", layer=j)
    o_p, st_p = hgrn_scan(proj, lb_raw, g_o, layer, n_batch, seq, scan_tb)
    o_s, st_s = hgrn_step(proj[Mp:], state, lb_raw, g_o, layer)
    o = jnp.concatenate([o_p, o_s], axis=0)
    x = mm(o, w_o_all, F32, "hgrn_out", res=x, scale=1.0, layer=j, tn=1024)
    return x, st_p, st_s


def _gmlp_mix_kernel(u_ref, v_ref, gv_ref, ws_ref, bias_ref, vo_ref, z_ref):
    v = _rms(v_ref[...], gv_ref[...])
    vo_ref[...] = v
    vb = v.astype(BF16)
    L = v.shape[0]
    r_id = lax.broadcasted_iota(jnp.int32, (L, L), 0)
    c_id = lax.broadcasted_iota(jnp.int32, (L, L), 1)
    for g in range(C_GROUPS):
        sl = slice(g * C_GDIM, (g + 1) * C_GDIM)
        w = jnp.where(r_id >= c_id, ws_ref[g], 0.0).astype(BF16)
        mixed = jnp.dot(w, vb[:, sl], preferred_element_type=F32) + bias_ref[:, sl]
        z_ref[:, sl] = (u_ref[:, sl] * mixed).astype(z_ref.dtype)


def gmlp_mix(uv, g_v, w_s, bias_full, n_rows):
    W = g_v.shape[0]
    L = C_CHUNK
    return pl.pallas_call(
        _gmlp_mix_kernel,
        grid=(n_rows // L,),
        in_specs=[pl.BlockSpec((L, W), lambda c: (c, 0)),
                  pl.BlockSpec((L, W), lambda c: (c, 1)),
                  pl.BlockSpec((1, W), lambda c: (0, 0)),
                  pl.BlockSpec((C_GROUPS, L, L), lambda c: (0, 0, 0)),
                  pl.BlockSpec((L, W), lambda c: (0, 0))],
        out_specs=[pl.BlockSpec((L, W), lambda c: (c, 0)),
                   pl.BlockSpec((L, W), lambda c: (c, 0))],
        out_shape=[jax.ShapeDtypeStruct((n_rows, W), F32),
                   jax.ShapeDtypeStruct((n_rows, W), BF16)],
        compiler_params=_cparams(("parallel",)),
        name="gmlp_mix",
    )(uv, uv, g_v.reshape(1, W), w_s, bias_full)


def _gmlp_single_kernel(u_ref, v_ref, gv_ref, w0_ref, b0_ref, vo_ref, z_ref):
    v = _rms(v_ref[...], gv_ref[...])
    vo_ref[...] = v
    z_ref[...] = (u_ref[...] * (w0_ref[...] * v + b0_ref[...])).astype(z_ref.dtype)


def gmlp_single(uv_s, g_v, w0, b0):
    R = uv_s.shape[0]
    W = g_v.shape[0]
    blk = lambda j: pl.BlockSpec((R, W), lambda i, j=j: (0, j))
    vec = pl.BlockSpec((1, W), lambda i: (0, 0))
    return pl.pallas_call(
        _gmlp_single_kernel,
        grid=(1,),
        in_specs=[blk(0), blk(1), vec, vec, vec],
        out_specs=[blk(0), blk(0)],
        out_shape=[jax.ShapeDtypeStruct((R, W), F32), jax.ShapeDtypeStruct((R, W), BF16)],
        compiler_params=_cparams(("arbitrary",)),
        name="gmlp_single",
    )(uv_s, uv_s, g_v.reshape(1, W), w0, b0)


def gmlp_layer(x, g_mix, w_in_all, g_v, w_s, b_s, w_o_all, j, n_batch, seq):
    Mp = n_batch * seq
    W = g_v.shape[0]
    uv = norm_mm(x, g_mix, w_in_all, (0,), w_in_all.shape[-1], 1024, _gelu_tanh, F32, "gmlp_in", layer=j)
    assert seq % C_CHUNK == 0
    bias_full = jnp.repeat(b_s[:, :C_CHUNK].T, C_GDIM, axis=1)
    v_p, z_p = gmlp_mix(uv, g_v, w_s[:, :C_CHUNK, :C_CHUNK], bias_full, Mp)
    w0 = jnp.repeat(w_s[:, 0, 0], C_GDIM).reshape(1, W)
    b0 = jnp.repeat(b_s[:, 0], C_GDIM).reshape(1, W)
    v_s, z_s = gmlp_single(uv[Mp:], g_v, w0, b0)
    z = jnp.concatenate([z_p, z_s], axis=0)
    x = mm(z, w_o_all, F32, "gmlp_out", res=x, scale=1.0, layer=j, tn=1024)
    last = ((seq - 1) // C_CHUNK) * C_CHUNK
    v_last = v_p.reshape(n_batch, seq, W)[:, last:]
    return x, v_last, v_s


def _forward(x_prompt, x_sample, cache_a_ckv, cache_a_kr, state_b, page_table, norm_ffn_a,
             ffn_a_wi, ffn_a_wo, norm_mix, a_w_in, a_g_q, a_g_kv, a_w_uq, a_w_ukv, a_w_o,
             b_w_in, b_lower_bounds, b_g_o, b_w_o, c_w_in, c_g_v, c_w_s, c_b_s, c_w_o,
             norm_ffn_b, ffn_b_wi, ffn_b_wo, final_norm, *, flash_tile, scan_tb):
    n_batch, seq, D = x_prompt.shape
    Bd, dec_seq, _ = x_sample.shape
    assert dec_seq == 1
    depth = norm_mix.shape[0]
    Mp = n_batch * seq
    past_len = page_table.shape[1] * PAGE_SIZE
    x = jnp.concatenate([x_prompt.reshape(Mp, D), x_sample.reshape(Bd, D)], axis=0)

    pos = jnp.concatenate([jnp.tile(jnp.arange(seq), n_batch),
                           jnp.full((Bd,), past_len, jnp.int32)])
    cos_t, sin_t = _rope_tables(pos)

    cache_krt = jnp.swapaxes(cache_a_kr, 2, 3)

    a_ckv, a_kr, b_sp, b_ss, c_vp, c_vs = [], [], [], [], [], []
    a_w_o, b_w_in, b_w_o, c_w_in, c_w_o = (w.astype(BF16) for w in (a_w_o, b_w_in, b_w_o, c_w_in, c_w_o))

    for i in range(depth):
        x = ffn_half_step(x, norm_ffn_a[i], ffn_a_wi, ffn_a_wo, i)
        j = i // N_MIXERS
        kind = i % N_MIXERS
        if kind == 0:
            x, ckv, kr = mla_layer(x, norm_mix[i], a_w_in[j], a_g_q[j], a_g_kv[j], a_w_uq[j],
                                   a_w_ukv[j], a_w_o, cache_a_ckv, cache_krt, page_table,
                                   j, n_batch, seq, cos_t, sin_t, flash_tile)
            a_ckv.append(ckv)
            a_kr.append(kr)
        elif kind == 1:
            x, sp, ss = hgrn_layer(x, norm_mix[i], b_w_in, b_lower_bounds, b_g_o[j], b_w_o,
                                   state_b[j], j, i, n_batch, seq, scan_tb)
            b_sp.append(sp)
            b_ss.append(ss)
        else:
            x, vp, vs = gmlp_layer(x, norm_mix[i], c_w_in, c_g_v[j], c_w_s[j], c_b_s[j],
                                   c_w_o, j, n_batch, seq)
            c_vp.append(vp)
            c_vs.append(vs)
        x = ffn_half_step(x, norm_ffn_b[i], ffn_b_wi, ffn_b_wo, i)

    y_p = final_norm_call(x, final_norm, 0, Mp, min(1024, Mp))
    y_s = final_norm_call(x, final_norm, Mp, Bd, Bd)
    ckv_all = jnp.stack(a_ckv)
    kr_all = jnp.stack(a_kr)
    return (y_p.reshape(n_batch, seq, D), y_s.reshape(Bd, 1, D),
            ckv_all[:, :Mp].reshape(-1, n_batch, seq, A_KV_LORA),
            kr_all[:, :Mp].reshape(-1, n_batch, seq, A_ROPE),
            ckv_all[:, Mp:].reshape(-1, Bd, 1, A_KV_LORA),
            kr_all[:, Mp:].reshape(-1, Bd, 1, A_ROPE),
            jnp.stack(b_sp), jnp.stack(b_ss),
            jnp.stack(c_vp), jnp.stack(c_vs)[:, :, None, :])


def kernel(x_prompt, x_sample, cache_a_ckv, cache_a_kr, state_b, page_table, norm_ffn_a, ffn_a_wi, ffn_a_wo, norm_mix, a_w_in, a_g_q, a_g_kv, a_w_uq, a_w_ukv, a_w_o, b_w_in, b_lower_bounds, b_g_o, b_w_o, c_w_in, c_g_v, c_w_s, c_b_s, c_w_o, norm_ffn_b, ffn_b_wi, ffn_b_wo, final_norm):
    seq = x_prompt.shape[1]
    return _forward(x_prompt, x_sample, cache_a_ckv, cache_a_kr, state_b, page_table, norm_ffn_a,
                    ffn_a_wi, ffn_a_wo, norm_mix, a_w_in, a_g_q, a_g_kv, a_w_uq, a_w_ukv, a_w_o,
                    b_w_in, b_lower_bounds, b_g_o, b_w_o, c_w_in, c_g_v, c_w_s, c_b_s, c_w_o,
                    norm_ffn_b, ffn_b_wi, ffn_b_wo, final_norm,
                    flash_tile=min(1024, seq // 2), scan_tb=min(256, seq))
```

```python
import functools
import math

import jax
import jax.numpy as jnp
from jax import lax
from jax.experimental import pallas as pl
from jax.experimental.pallas import tpu as pltpu

F32 = jnp.float32
BF16 = jnp.bfloat16

EPS = 1e-6
ROPE_THETA = 10000.0
N_MIXERS = 3

A_HEADS = 16
A_NOPE = 128
A_ROPE = 64
A_VDIM = 128
A_Q_LORA = 512
A_KV_LORA = 512
A_QK_PAD = 256
Q_PRESCALE = (A_NOPE + A_ROPE) ** -0.5 * math.log2(math.e)
PAGE_SIZE = 128
N_DECODE_SLOTS = 4

B_HEADS = 16
B_DK = 128
B_DV = 128

C_GROUPS = 16
C_GDIM = 128
C_CHUNK = 128

MIB = 1024 * 1024


def _cparams(semantics, vmem_mib=48):
    return pltpu.CompilerParams(dimension_semantics=semantics,
                                vmem_limit_bytes=vmem_mib * MIB)


def _rms(x, g):
    return x * lax.rsqrt(jnp.mean(x * x, axis=-1, keepdims=True) + EPS) * g


def _sigmoid(x):
    return 1.0 / (1.0 + jnp.exp(-x))


def _silu(x):
    return x * _sigmoid(x)


def _gelu_tanh(x):
    c = math.sqrt(2.0 / math.pi)
    return 0.5 * x * (1.0 + jnp.tanh(c * (x + 0.044715 * (x * x * x))))


def _row_tile(m, cap):
    best = None
    for t in range(16, cap + 1, 16):
        if m % t == 0:
            best = t
    assert best is not None, (m, cap)
    return best


def _norm_mm_kernel(x_ref, g_ref, *refs, n_w, epilogue):
    w_refs = refs[:n_w]
    o_ref = refs[n_w]
    h_ref = refs[n_w + 1]

    @pl.when(pl.program_id(1) == 0)
    def _():
        h_ref[...] = _rms(x_ref[...], g_ref[...]).astype(BF16)

    h = h_ref[...]
    accs = [jnp.dot(h, w[...].astype(BF16), preferred_element_type=F32) for w in w_refs]
    o_ref[...] = epilogue(*accs).astype(o_ref.dtype)


def _weight_spec(w, layer, rows, tn, col_block):
    if layer is None:
        assert w.ndim == 2
        return pl.BlockSpec((rows, tn), lambda i, j: (0, col_block(i, j)))
    assert w.ndim == 3
    return pl.BlockSpec((None, rows, tn), lambda i, j: (layer, 0, col_block(i, j)))


def norm_mm(x, g, w, col_offsets, out_cols, tn, epilogue, out_dtype, name, layer=None, tm_cap=1040):
    M, D = x.shape
    tm = _row_tile(M, tm_cap)
    n_w = len(col_offsets)
    in_specs = [pl.BlockSpec((tm, D), lambda i, j: (i, 0)),
                pl.BlockSpec((1, D), lambda i, j: (0, 0))]
    for off in col_offsets:
        assert off % tn == 0
        in_specs.append(_weight_spec(w, layer, D, tn, lambda i, j, o=off // tn: j + o))
    return pl.pallas_call(
        functools.partial(_norm_mm_kernel, n_w=n_w, epilogue=epilogue),
        grid=(M // tm, out_cols // tn),
        in_specs=in_specs,
        out_specs=pl.BlockSpec((tm, tn), lambda i, j: (i, j)),
        out_shape=jax.ShapeDtypeStruct((M, out_cols), out_dtype),
        scratch_shapes=[pltpu.VMEM((tm, D), BF16)],
        compiler_params=_cparams(("parallel", "arbitrary")),
        name=name,
    )(x, g.reshape(1, D), *([w] * n_w))


def _mm_kernel(a_ref, w_ref, *refs, scale, has_res):
    o_ref = refs[-1]
    acc = jnp.dot(a_ref[...], w_ref[...].astype(BF16), preferred_element_type=F32)
    if has_res:
        acc = refs[0][...] + scale * acc
    o_ref[...] = acc.astype(o_ref.dtype)


def mm(a, w, out_dtype, name, res=None, scale=1.0, layer=None, tm_cap=1040, tn=512, vmem_mib=48):
    M, K = a.shape
    N = w.shape[-1]
    tm = _row_tile(M, tm_cap)
    tn = min(tn, N)
    in_specs = [pl.BlockSpec((tm, K), lambda i, j: (i, 0)),
                _weight_spec(w, layer, K, tn, lambda i, j: j)]
    args = [a, w]
    if res is not None:
        in_specs.append(pl.BlockSpec((tm, tn), lambda i, j: (i, j)))
        args.append(res)
    return pl.pallas_call(
        functools.partial(_mm_kernel, scale=scale, has_res=res is not None),
        grid=(M // tm, N // tn),
        in_specs=in_specs,
        out_specs=pl.BlockSpec((tm, tn), lambda i, j: (i, j)),
        out_shape=jax.ShapeDtypeStruct((M, N), out_dtype),
        compiler_params=_cparams(("parallel", "parallel"), vmem_mib),
        name=name,
    )(*args)


def _head_mm_kernel(x_ref, w_ref, o_ref, *, heads, k, n):
    for h in range(heads):
        o_ref[:, h * n:(h + 1) * n] = jnp.dot(
            x_ref[:, h * k:(h + 1) * k], w_ref[h], preferred_element_type=F32).astype(o_ref.dtype)


def head_mm(x, w, out_dtype, name):
    R = x.shape[0]
    H, K, N = w.shape
    return pl.pallas_call(
        functools.partial(_head_mm_kernel, heads=H, k=K, n=N),
        grid=(1,),
        in_specs=[pl.BlockSpec((R, H * K), lambda i: (0, 0)),
                  pl.BlockSpec((H, K, N), lambda i: (0, 0, 0))],
        out_specs=pl.BlockSpec((R, H * N), lambda i: (0, 0)),
        out_shape=jax.ShapeDtypeStruct((R, H * N), out_dtype),
        compiler_params=_cparams(("arbitrary",)),
        name=name,
    )(x, w)


def _norm_kernel(x_ref, g_ref, o_ref):
    o_ref[...] = _rms(x_ref[...], g_ref[...])


def final_norm_call(x, g, row0, n_rows, tm):
    D = x.shape[1]
    assert row0 % tm == 0 and n_rows % tm == 0
    blk0 = row0 // tm
    return pl.pallas_call(
        _norm_kernel,
        grid=(n_rows // tm,),
        in_specs=[pl.BlockSpec((tm, D), lambda i: (i + blk0, 0)),
                  pl.BlockSpec((1, D), lambda i: (0, 0))],
        out_specs=pl.BlockSpec((tm, D), lambda i: (i, 0)),
        out_shape=jax.ShapeDtypeStruct((n_rows, D), F32),
        compiler_params=_cparams(("parallel",)),
        name="final_norm",
    )(x, g.reshape(1, D))


def _swiglu_epilogue(gate, up):
    return _silu(gate) * up


def ffn_half_step(x, g, wi, wo, layer):
    d_ff = wo.shape[1]
    hid = norm_mm(x, g, wi, (0, d_ff), d_ff, 512, _swiglu_epilogue, BF16, "ffn_in", layer=layer)
    return mm(hid, wo, F32, "ffn_out", res=x, scale=0.5, layer=layer, tm_cap=1040, tn=256, vmem_mib=56)


def _mla_in_kernel(x_ref, g_ref, w_ref, gq_ref, gkv_ref, cos_ref, sin_ref,
                   cq_ref, ckv_ref, ckvb_ref, kr_ref, krb_ref):
    h = _rms(x_ref[...], g_ref[...]).astype(BF16)
    acc = jnp.dot(h, w_ref[...], preferred_element_type=F32)
    ql, kl = A_Q_LORA, A_KV_LORA
    cq_ref[...] = _rms(acc[:, :ql], gq_ref[...]).astype(BF16)
    ckv = _rms(acc[:, ql:ql + kl], gkv_ref[...])
    ckv_ref[...] = ckv
    ckvb_ref[...] = ckv.astype(BF16)
    kr = (acc[:, ql + kl:ql + kl + 128] * cos_ref[...]
          + acc[:, ql + kl + 128:ql + kl + 256] * sin_ref[...])
    kr_ref[...] = kr
    krb_ref[...] = kr.astype(BF16)


def mla_in(x, g, w_ext, g_q, g_kv, cos_t, sin_t):
    M, D = x.shape
    tm = _row_tile(M, 640)
    NW = w_ext.shape[1]
    row = lambda n: pl.BlockSpec((tm, n), lambda i: (i, 0))
    full = lambda a, b: pl.BlockSpec((a, b), lambda i: (0, 0))
    return pl.pallas_call(
        _mla_in_kernel,
        grid=(M // tm,),
        in_specs=[row(D), full(1, D), full(D, NW), full(1, A_Q_LORA), full(1, A_KV_LORA),
                  row(128), row(128)],
        out_specs=[row(A_Q_LORA), row(A_KV_LORA), row(A_KV_LORA), row(128), row(128)],
        out_shape=[jax.ShapeDtypeStruct((M, A_Q_LORA), BF16),
                   jax.ShapeDtypeStruct((M, A_KV_LORA), F32),
                   jax.ShapeDtypeStruct((M, A_KV_LORA), BF16),
                   jax.ShapeDtypeStruct((M, 128), F32),
                   jax.ShapeDtypeStruct((M, 128), BF16)],
        compiler_params=_cparams(("parallel",)),
        name="mla_in",
    )(x, g.reshape(1, D), w_ext, g_q.reshape(1, -1), g_kv.reshape(1, -1), cos_t, sin_t)


def _mla_q_kernel(cq_ref, w1_ref, w2_ref, cos_ref, sin_ref, o_ref):
    cq = cq_ref[...]
    cos = cos_ref[...]
    sin = sin_ref[...]
    P = A_QK_PAD
    for h in range(A_HEADS):
        a1 = jnp.dot(cq, w1_ref[:, h * P:(h + 1) * P], preferred_element_type=F32)
        a2 = jnp.dot(cq, w2_ref[:, h * 128:(h + 1) * 128], preferred_element_type=F32)
        o_ref[:, h * P:h * P + 128] = (a1[:, :128] * Q_PRESCALE).astype(BF16)
        o_ref[:, h * P + 128:(h + 1) * P] = ((a1[:, 128:] * cos + a2 * sin) * Q_PRESCALE).astype(BF16)


def mla_q(cq, w1, w2, cos_t, sin_t):
    M = cq.shape[0]
    tm = _row_tile(M, 640)
    row = lambda n: pl.BlockSpec((tm, n), lambda i: (i, 0))
    full = lambda a, b: pl.BlockSpec((a, b), lambda i: (0, 0))
    return pl.pallas_call(
        _mla_q_kernel,
        grid=(M // tm,),
        in_specs=[row(A_Q_LORA), full(*w1.shape), full(*w2.shape), row(128), row(128)],
        out_specs=row(A_HEADS * A_QK_PAD),
        out_shape=jax.ShapeDtypeStruct((M, A_HEADS * A_QK_PAD), BF16),
        compiler_params=_cparams(("parallel",)),
        name="mla_q",
    )(cq, w1, w2, cos_t, sin_t)


def _mla_kv_kernel(c_ref, wk_ref, wvt_ref, kn_ref, vt_ref):
    c = c_ref[...]
    kn_ref[...] = jnp.dot(c, wk_ref[...], preferred_element_type=F32).astype(BF16)
    vt_ref[...] = lax.dot_general(wvt_ref[...], c, (((1,), (1,)), ((), ())),
                                  preferred_element_type=F32).astype(BF16)


def mla_kv(ckvb, w_uk_cols, w_uv_rows, n_rows, tm):
    C = A_KV_LORA
    N = w_uk_cols.shape[1]
    NV = w_uv_rows.shape[0]
    return pl.pallas_call(
        _mla_kv_kernel,
        grid=(n_rows // tm,),
        in_specs=[pl.BlockSpec((tm, C), lambda i: (i, 0)),
                  pl.BlockSpec((C, N), lambda i: (0, 0)),
                  pl.BlockSpec((N, C), lambda i: (0, 0))],
        out_specs=[pl.BlockSpec((tm, N), lambda i: (i, 0)),
                   pl.BlockSpec((NV, tm), lambda i: (0, i))],
        out_shape=[jax.ShapeDtypeStruct((n_rows, N), BF16),
                   jax.ShapeDtypeStruct((NV, n_rows), BF16)],
        compiler_params=_cparams(("parallel",)),
        name="mla_kv",
    )(ckvb, w_uk_cols, w_uv_rows)


def _flash_kernel(q_ref, kn_ref, kr_ref, vt_ref, o_ref, m_ref, l_ref, acc_ref, *, half, n_pairs):
    nt_dims = (((1,), (1,)), ((), ()))

    def update(j, q0, k0, klen, masked):
        q = q_ref[pl.ds(q0, half), :]
        k = jnp.concatenate([kn_ref[pl.ds(k0, klen), :], kr_ref[pl.ds(k0, klen), :]], axis=1)
        st = lax.dot_general(k, q, nt_dims, preferred_element_type=F32)
        if masked:
            kpos = lax.broadcasted_iota(jnp.int32, st.shape, 0)
            qpos = lax.broadcasted_iota(jnp.int32, st.shape, 1)
            st = jnp.where(kpos <= qpos, st, -jnp.inf)
        m_prev = m_ref[j]
        m_new = jnp.maximum(m_prev, jnp.max(st, axis=0, keepdims=True))
        alpha = jnp.exp2(m_prev - m_new)
        p = jnp.exp2(st - m_new)
        l_ref[j] = alpha * l_ref[j] + jnp.sum(p, axis=0, keepdims=True)
        acc_ref[j] = alpha * acc_ref[j] + jnp.dot(vt_ref[:, pl.ds(k0, klen)], p.astype(BF16),
                                                  preferred_element_type=F32)
        m_ref[j] = m_new

    def finish(j):
        return (acc_ref[j] / l_ref[j]).T.astype(o_ref.dtype)

    def pair(p, carry):
        qa = pl.multiple_of(p * (2 * half), 2 * half)
        qb = pl.multiple_of(qa + half, half)
        m_ref[...] = jnp.full(m_ref.shape, -jnp.inf, F32)
        l_ref[...] = jnp.zeros(l_ref.shape, F32)
        acc_ref[...] = jnp.zeros(acc_ref.shape, F32)

        def k_step(kc, c):
            k0 = pl.multiple_of(kc * (2 * half), 2 * half)
            update(0, qa, k0, 2 * half, False)
            update(1, qb, k0, 2 * half, False)
            return c

        lax.fori_loop(0, p, k_step, 0)
        update(0, qa, qa, half, True)
        update(1, qb, qa, half, False)
        update(1, qb, qb, half, True)
        o_ref[pl.ds(qa, half), :] = finish(0)
        o_ref[pl.ds(qb, half), :] = finish(1)
        return carry

    lax.fori_loop(0, n_pairs, pair, 0)


def mla_flash(qp, kn, krb, vt, n_batch, seq, half):
    H = A_HEADS
    assert seq % (2 * half) == 0
    return pl.pallas_call(
        functools.partial(_flash_kernel, half=half, n_pairs=seq // (2 * half)),
        grid=(n_batch, H),
        in_specs=[pl.BlockSpec((seq, A_QK_PAD), lambda b, h: (b, h)),
                  pl.BlockSpec((seq, A_NOPE), lambda b, h: (b, h)),
                  pl.BlockSpec((seq, 128), lambda b, h: (b, 0)),
                  pl.BlockSpec((A_VDIM, seq), lambda b, h: (h, b))],
        out_specs=pl.BlockSpec((seq, A_VDIM), lambda b, h: (b, h)),
        out_shape=jax.ShapeDtypeStruct((n_batch * seq, H * A_VDIM), BF16),
        scratch_shapes=[pltpu.VMEM((2, 1, half), F32), pltpu.VMEM((2, 1, half), F32),
                        pltpu.VMEM((2, A_VDIM, half), F32)],
        compiler_params=_cparams(("parallel", "parallel")),
        name="mla_flash",
    )(qp, kn, krb, vt)


def _decode_kernel(pt_ref, ql_ref, qr_ref, cn_ref, rn_ref, ckv_hbm, krt_hbm, o_ref,
                   cbuf, rbuf, sems, m_ref, l_ref, acc_ref,
                   *, layer, n_pages, chunk, group):
    g = pl.program_id(0)
    n_chunks = n_pages // chunk
    last = 2 * pl.num_programs(0) - 1
    c0 = (2 * g) % n_chunks

    def copies(k, slot):
        out = []
        rg = k // n_chunks
        cc = k % n_chunks
        for r in range(group):
            base = (rg * group + r) * n_pages + cc * chunk
            for p in range(chunk):
                page = pt_ref[base + p]
                out.append(pltpu.make_async_copy(ckv_hbm.at[layer, page], cbuf.at[slot, r, p],
                                                 sems.at[0, slot]))
                out.append(pltpu.make_async_copy(krt_hbm.at[layer, page], rbuf.at[slot, r, p],
                                                 sems.at[1, slot]))
        return out

    def start(k, slot):
        for cp in copies(k, slot):
            cp.start()

    def wait(k, slot):
        for cp in copies(k, slot):
            cp.wait()

    s0 = (2 * g) % N_DECODE_SLOTS

    @pl.when(g == 0)
    def _():
        start(0, 0)
        start(1, 1)

    @pl.when(c0 == 0)
    def _():
        for r in range(group):
            cn = cn_ref[r].astype(F32)
            rn = rn_ref[r].astype(F32)
            s_new = (jnp.sum(ql_ref[r].astype(F32) * cn, axis=-1, keepdims=True)
                     + jnp.sum(qr_ref[r].astype(F32) * rn, axis=-1, keepdims=True))
            m_ref[r] = s_new
            l_ref[r] = jnp.ones(s_new.shape, F32)
            acc_ref[r] = jnp.broadcast_to(cn, acc_ref.shape[1:])

    nt_dims = (((1,), (1,)), ((), ()))

    def attend(slot):
        cbs, scores = [], []
        for r in range(group):
            cb = cbuf[slot, r].reshape(chunk * PAGE_SIZE, A_KV_LORA).astype(BF16)
            s_rope = jnp.concatenate(
                [jnp.dot(qr_ref[r], rbuf[slot, r, p].astype(BF16), preferred_element_type=F32)
                 for p in range(chunk)], axis=1)
            cbs.append(cb)
            scores.append(lax.dot_general(ql_ref[r], cb, nt_dims, preferred_element_type=F32) + s_rope)
        for r in range(group):
            s = scores[r]
            m_prev = m_ref[r]
            m_new = jnp.maximum(m_prev, jnp.max(s, axis=-1, keepdims=True))
            alpha = jnp.exp2(m_prev - m_new)
            p = jnp.exp2(s - m_new)
            l_ref[r] = alpha * l_ref[r] + jnp.sum(p, axis=-1, keepdims=True)
            acc_ref[r] = alpha * acc_ref[r] + jnp.dot(p.astype(BF16), cbs[r], preferred_element_type=F32)
            m_ref[r] = m_new

    start(jnp.minimum(2 * g + 2, last), (s0 + 2) % N_DECODE_SLOTS)
    wait(2 * g, s0)
    attend(s0)
    start(jnp.minimum(2 * g + 3, last), (s0 + 3) % N_DECODE_SLOTS)
    wait(2 * g + 1, s0 + 1)
    attend(s0 + 1)

    @pl.when(c0 + 1 == n_chunks - 1)
    def _():
        for r in range(group):
            o_ref[r] = (acc_ref[r] / l_ref[r]).astype(o_ref.dtype)

    @pl.when(2 * g + 1 == last)
    def _():
        wait(last, (s0 + 2) % N_DECODE_SLOTS)
        wait(last, (s0 + 3) % N_DECODE_SLOTS)


def mla_decode(page_table, ql, qr, cn, rn, cache_ckv, cache_krt, layer, chunk=8, group=2):
    Bd, n_pages = page_table.shape
    assert n_pages % (2 * chunk) == 0 and Bd % group == 0
    steps_per_group = n_pages // (2 * chunk)
    H, C, R = A_HEADS, A_KV_LORA, A_ROPE
    req = lambda shp: pl.BlockSpec((group,) + shp, lambda g, pt: (g // steps_per_group, 0, 0))
    grid_spec = pltpu.PrefetchScalarGridSpec(
        num_scalar_prefetch=1,
        grid=(Bd // group * steps_per_group,),
        in_specs=[req((H, C)), req((H, R)), req((1, C)), req((1, R)),
                  pl.BlockSpec(memory_space=pl.ANY), pl.BlockSpec(memory_space=pl.ANY)],
        out_specs=req((H, C)),
        scratch_shapes=[pltpu.VMEM((N_DECODE_SLOTS, group, chunk, PAGE_SIZE, C), F32),
                        pltpu.VMEM((N_DECODE_SLOTS, group, chunk, R, PAGE_SIZE), F32),
                        pltpu.SemaphoreType.DMA((2, N_DECODE_SLOTS)),
                        pltpu.VMEM((group, H, 1), F32), pltpu.VMEM((group, H, 1), F32),
                        pltpu.VMEM((group, H, C), F32)])
    return pl.pallas_call(
        functools.partial(_decode_kernel, layer=layer, n_pages=n_pages, chunk=chunk, group=group),
        grid_spec=grid_spec,
        out_shape=jax.ShapeDtypeStruct((Bd, H, C), BF16),
        compiler_params=_cparams(("arbitrary",)),
        name="mla_decode",
    )(page_table.reshape(-1), ql, qr, cn, rn, cache_ckv, cache_krt)


def _rope_tables(pos):
    half = A_ROPE // 2
    inv = ROPE_THETA ** (-jnp.arange(half, dtype=F32) / half)
    ang = pos.astype(F32)[:, None] * inv[None, :]
    cos, sin = jnp.cos(ang), jnp.sin(ang)
    z = jnp.zeros((pos.shape[0], 128 - A_ROPE), F32)
    return (jnp.concatenate([cos, cos, z], axis=1), jnp.concatenate([sin, sin, z], axis=1))


def _half_swap(w):
    half = A_ROPE // 2
    return jnp.concatenate([-w[..., half:], w[..., :half]], axis=-1)


def mla_layer(x, g_mix, w_in, g_q, g_kv, w_uq, w_ukv, w_o_all, cache_ckv, cache_krt, page_table,
              layer, n_batch, seq, cos_t, sin_t, flash_tile):
    D = x.shape[1]
    Mp = n_batch * seq
    H = A_HEADS
    lo = A_Q_LORA + A_KV_LORA
    zpad = jnp.zeros((D, 128 - A_ROPE), F32)
    w_kr = w_in[:, lo:]
    w_in_ext = jnp.concatenate([w_in[:, :lo], w_kr, zpad, _half_swap(w_kr), zpad], axis=1).astype(BF16)
    cq, ckv, ckvb, kr, krb = mla_in(x, g_mix, w_in_ext, g_q, g_kv, cos_t, sin_t)

    wq = w_uq.reshape(A_Q_LORA, H, A_NOPE + A_ROPE)
    wq_rope = wq[..., A_NOPE:]
    zq = jnp.zeros((A_Q_LORA, H, 128 - A_ROPE), F32)
    w1 = jnp.concatenate([wq[..., :A_NOPE], wq_rope, zq], axis=-1).reshape(A_Q_LORA, H * A_QK_PAD)
    w2 = jnp.concatenate([_half_swap(wq_rope), zq], axis=-1).reshape(A_Q_LORA, H * 128)
    qp = mla_q(cq, w1.astype(BF16), w2.astype(BF16), cos_t, sin_t)

    wkv = w_ukv.reshape(A_KV_LORA, H, A_NOPE + A_VDIM)
    w_uk = wkv[..., :A_NOPE]
    w_uv = wkv[..., A_NOPE:]
    kn, vt = mla_kv(ckvb, w_uk.reshape(A_KV_LORA, H * A_NOPE).astype(BF16),
                    jnp.transpose(w_uv, (1, 2, 0)).reshape(H * A_VDIM, A_KV_LORA).astype(BF16),
                    Mp, min(1024, seq))
    o_p = mla_flash(qp, kn, krb, vt, n_batch, seq, flash_tile)

    Bd = page_table.shape[0]
    qs = qp[Mp:].reshape(Bd, H, A_QK_PAD)
    q_nope = qs[:, :, :A_NOPE].reshape(Bd, H * A_NOPE)
    q_rope = qs[:, :, A_NOPE:A_NOPE + A_ROPE]
    q_lat = head_mm(q_nope, jnp.transpose(w_uk, (1, 2, 0)).astype(BF16), BF16, "mla_qlat")
    o_lat = mla_decode(page_table, q_lat.reshape(Bd, H, A_KV_LORA), q_rope,
                       ckvb[Mp:].reshape(Bd, 1, A_KV_LORA),
                       krb[Mp:, :A_ROPE].reshape(Bd, 1, A_ROPE),
                       cache_ckv, cache_krt, layer)
    o_s = head_mm(o_lat.reshape(Bd, H * A_KV_LORA), jnp.transpose(w_uv, (1, 0, 2)).astype(BF16), BF16,
                  "mla_ouv")

    o = jnp.concatenate([o_p, o_s], axis=0)
    x = mm(o, w_o_all, F32, "mla_out", res=x, scale=1.0, layer=layer, tn=1024)
    return x, ckv, kr[:, :A_ROPE]


def _lower_bound(raw, layer):
    e = jnp.exp(raw - jnp.max(raw, axis=0, keepdims=True))
    tot = jnp.sum(e, axis=0, keepdims=True)
    if layer == 0:
        return jnp.zeros_like(tot)
    return jnp.sum(e[1:layer + 1], axis=0, keepdims=True) / tot


def _hgrn_scan_kernel(q_ref, f_ref, i_ref, og_ref, lbr_ref, go_ref,
                      seg_ref, pm_ref, rm_ref, o_ref, st_ref, state_ref, a_ref, *, layer, tb, hb):
    t = pl.program_id(2)
    n_lev = pm_ref.shape[0]
    tn_dims = (((0,), (0,)), ((), ()))
    nt_dims = (((1,), (1,)), ((), ()))

    @pl.when(t == 0)
    def _():
        state_ref[...] = jnp.zeros(state_ref.shape, F32)

    lb = _lower_bound(lbr_ref[...], layer)
    gate = lb + (1.0 - lb) * _sigmoid(f_ref[...])
    logf = jnp.log(gate)
    q = _silu(q_ref[...])
    k = 1.0 - gate
    vb = i_ref[...].astype(BF16)
    a1 = logf.astype(BF16)
    r1 = logf - a1.astype(F32)
    a2 = r1.astype(BF16)
    a3 = (r1 - a2.astype(F32)).astype(BF16)

    def seg_sum(lev):
        m = seg_ref[lev]
        return (jnp.dot(m, a1, preferred_element_type=F32) + jnp.dot(m, a2, preferred_element_type=F32)
                + jnp.dot(m, a3, preferred_element_type=F32))

    b = seg_sum(n_lev)
    width = b.shape[1]

    for lev in range(n_lev):
        right = jnp.concatenate([rm_ref[lev]] * hb, axis=1)
        blk = tb >> lev
        if blk >= 8:
            b3 = b.reshape(tb // blk, blk, width)
            b_split = jnp.broadcast_to(b3[:, blk // 2 - 1:blk // 2, :], b3.shape).reshape(tb, width)
            x = -jnp.abs(b - b_split)
        else:
            x = seg_sum(lev)
        e = jnp.exp(x)
        qh = (q * e * right).astype(BF16)
        kh = (k * e * (1.0 - right)).astype(BF16)
        for hh in range(hb):
            ls = slice(hh * B_DK, (hh + 1) * B_DK)
            s = lax.dot_general(qh[:, ls], kh[:, ls], nt_dims, preferred_element_type=F32) * pm_ref[lev]
            if lev == 0:
                a_ref[hh] = s
            else:
                a_ref[hh] = a_ref[hh] + s

    b_last = b[tb - 1:tb, :]
    qe = (q * jnp.exp(b)).astype(BF16)
    kd = (k * jnp.exp(b_last - b)).astype(BF16)
    decay = jnp.exp(b_last)
    qk = q * k
    for hh in range(hb):
        ls = slice(hh * B_DK, (hh + 1) * B_DK)
        st = state_ref[hh]
        v = vb[:, ls]
        o = (lax.dot_general(qe[:, ls], st.astype(BF16), nt_dims, preferred_element_type=F32)
             + jnp.dot(a_ref[hh].astype(BF16), v, preferred_element_type=F32)
             + jnp.sum(qk[:, ls], axis=-1, keepdims=True) * i_ref[:, ls])
        state_ref[hh] = st * decay[:, ls] + lax.dot_general(v, kd[:, ls], tn_dims,
                                                            preferred_element_type=F32)
        o = _rms(o, go_ref[:, ls]) * _silu(og_ref[:, ls])
        o_ref[:, ls] = o.astype(o_ref.dtype)

    @pl.when(t == pl.num_programs(2) - 1)
    def _():
        for hh in range(hb):
            st_ref[0, hh] = state_ref[hh].T


def _scan_constants(tb):
    assert tb & (tb - 1) == 0
    t = jnp.arange(tb)[:, None]
    j = jnp.arange(tb)[None, :]
    segs, pms, rms = [], [], []
    h = tb // 2
    while h >= 1:
        split = (t // (2 * h)) * (2 * h) + h - 1
        right = t > split
        segs.append(jnp.where(right, (j > split) & (j <= t), (j > t) & (j <= split)))
        pms.append((t // (2 * h)) == (j // (2 * h)))
        rms.append(jnp.broadcast_to(right, (tb, B_DK)))
        h //= 2
    segs.append(j <= t)
    return (jnp.stack(segs).astype(BF16), jnp.stack(pms).astype(F32), jnp.stack(rms).astype(F32))


def hgrn_scan(proj, lb_raw, g_o, layer, n_batch, seq, tb, hb=8):
    H = B_HEADS
    assert H % hb == 0
    nt = seq // tb
    W = hb * B_DK
    seg, pm, rm = _scan_constants(tb)
    sec = lambda k: pl.BlockSpec((tb, W), lambda b, h, t, k=k: (b * nt + t, k * (H // hb) + h))
    const = lambda a: pl.BlockSpec(a.shape, lambda b, h, t: (0, 0, 0))
    return pl.pallas_call(
        functools.partial(_hgrn_scan_kernel, layer=layer, tb=tb, hb=hb),
        grid=(n_batch, H // hb, nt),
        in_specs=[sec(0), sec(1), sec(2), sec(3),
                  pl.BlockSpec((lb_raw.shape[0], W), lambda b, h, t: (0, h)),
                  pl.BlockSpec((1, W), lambda b, h, t: (0, h)),
                  const(seg), const(pm), const(rm)],
        out_specs=[pl.BlockSpec((tb, W), lambda b, h, t: (b * nt + t, h)),
                   pl.BlockSpec((1, hb, B_DK, B_DV), lambda b, h, t: (b, h, 0, 0))],
        out_shape=[jax.ShapeDtypeStruct((n_batch * seq, H * B_DV), BF16),
                   jax.ShapeDtypeStruct((n_batch, H, B_DK, B_DV), F32)],
        scratch_shapes=[pltpu.VMEM((hb, B_DV, B_DK), F32), pltpu.VMEM((hb, tb, tb), F32)],
        compiler_params=_cparams(("parallel", "parallel", "arbitrary")),
        name="hgrn_scan",
    )(proj, proj, proj, proj, lb_raw, g_o.reshape(1, -1), seg, pm, rm)


def _hgrn_step_kernel(p_ref, s_ref, lbr_ref, go_ref, o_ref, so_ref, *, layer):
    H = B_HEADS
    p = p_ref[0]
    lb = _lower_bound(lbr_ref[...], layer)[0]
    gate = lb + (1.0 - lb) * _sigmoid(p[H:2 * H])
    q = _silu(p[0:H])
    v = p[2 * H:3 * H]
    og = p[3 * H:4 * H]
    packed = jnp.concatenate([gate, jnp.zeros((128 - H, B_DK), F32)], axis=0)
    cols = packed.T
    qb = q.astype(BF16)
    outs = []
    for h in range(H):
        g_col = cols[:, h:h + 1]
        s_new = g_col * s_ref[0, h] + (1.0 - g_col) * v[h:h + 1, :]
        so_ref[0, h] = s_new
        outs.append(jnp.dot(qb, s_new.astype(BF16), preferred_element_type=F32)[h:h + 1, :])
    o = jnp.concatenate(outs, axis=0)
    o_ref[0] = (_rms(o, go_ref[...]) * _silu(og)).astype(o_ref.dtype)


def hgrn_step(proj_s, state, lb_raw, g_o, layer):
    Bd = proj_s.shape[0]
    H = B_HEADS
    depth = lb_raw.shape[0]
    o, s_new = pl.pallas_call(
        functools.partial(_hgrn_step_kernel, layer=layer),
        grid=(Bd,),
        in_specs=[pl.BlockSpec((1, 4 * H, 128), lambda r: (r, 0, 0)),
                  pl.BlockSpec((1, H, B_DK, B_DV), lambda r: (r, 0, 0, 0)),
                  pl.BlockSpec((depth, H, B_DK), lambda r: (0, 0, 0)),
                  pl.BlockSpec((H, B_DV), lambda r: (0, 0))],
        out_specs=[pl.BlockSpec((1, H, B_DV), lambda r: (r, 0, 0)),
                   pl.BlockSpec((1, H, B_DK, B_DV), lambda r: (r, 0, 0, 0))],
        out_shape=[jax.ShapeDtypeStruct((Bd, H, B_DV), BF16),
                   jax.ShapeDtypeStruct(state.shape, F32)],
        compiler_params=_cparams(("parallel",)),
        name="hgrn_step",
    )(proj_s.reshape(Bd, 4 * H, 128), state, lb_raw.reshape(depth, H, B_DK), g_o.reshape(H, B_DV))
    return o.reshape(Bd, H * B_DV), s_new


def hgrn_layer(x, g_mix, w_in_all, lb_raw, g_o, w_o_all, state, j, layer, n_batch, seq, scan_tb):
    Mp = n_batch * seq
    proj = norm_mm(x, g_mix, w_in_all, (0,), w_in_all.shape[-1], 1024, lambda a: a, F32, "hgrn_in", layer=j)
    o_p, st_p = hgrn_scan(proj, lb_raw, g_o, layer, n_batch, seq, scan_tb)
    o_s, st_s = hgrn_step(proj[Mp:], state, lb_raw, g_o, layer)
    o = jnp.concatenate([o_p, o_s], axis=0)
    x = mm(o, w_o_all, F32, "hgrn_out", res=x, scale=1.0, layer=j, tn=1024)
    return x, st_p, st_s


def _gmlp_mix_kernel(u_ref, v_ref, gv_ref, ws_ref, bias_ref, vo_ref, z_ref):
    v = _rms(v_ref[...], gv_ref[...])
    vo_ref[...] = v
    vb = v.astype(BF16)
    L = v.shape[0]
    r_id = lax.broadcasted_iota(jnp.int32, (L, L), 0)
    c_id = lax.broadcasted_iota(jnp.int32, (L, L), 1)
    for g in range(C_GROUPS):
        sl = slice(g * C_GDIM, (g + 1) * C_GDIM)
        w = jnp.where(r_id >= c_id, ws_ref[g], 0.0).astype(BF16)
        mixed = jnp.dot(w, vb[:, sl], preferred_element_type=F32) + bias_ref[:, sl]
        z_ref[:, sl] = (u_ref[:, sl] * mixed).astype(z_ref.dtype)


def gmlp_mix(uv, g_v, w_s, bias_full, n_rows):
    W = g_v.shape[0]
    L = C_CHUNK
    return pl.pallas_call(
        _gmlp_mix_kernel,
        grid=(n_rows // L,),
        in_specs=[pl.BlockSpec((L, W), lambda c: (c, 0)),
                  pl.BlockSpec((L, W), lambda c: (c, 1)),
                  pl.BlockSpec((1, W), lambda c: (0, 0)),
                  pl.BlockSpec((C_GROUPS, L, L), lambda c: (0, 0, 0)),
                  pl.BlockSpec((L, W), lambda c: (0, 0))],
        out_specs=[pl.BlockSpec((L, W), lambda c: (c, 0)),
                   pl.BlockSpec((L, W), lambda c: (c, 0))],
        out_shape=[jax.ShapeDtypeStruct((n_rows, W), F32),
                   jax.ShapeDtypeStruct((n_rows, W), BF16)],
        compiler_params=_cparams(("parallel",)),
        name="gmlp_mix",
    )(uv, uv, g_v.reshape(1, W), w_s, bias_full)


def _gmlp_single_kernel(u_ref, v_ref, gv_ref, w0_ref, b0_ref, vo_ref, z_ref):
    v = _rms(v_ref[...], gv_ref[...])
    vo_ref[...] = v
    z_ref[...] = (u_ref[...] * (w0_ref[...] * v + b0_ref[...])).astype(z_ref.dtype)


def gmlp_single(uv_s, g_v, w0, b0):
    R = uv_s.shape[0]
    W = g_v.shape[0]
    blk = lambda j: pl.BlockSpec((R, W), lambda i, j=j: (0, j))
    vec = pl.BlockSpec((1, W), lambda i: (0, 0))
    return pl.pallas_call(
        _gmlp_single_kernel,
        grid=(1,),
        in_specs=[blk(0), blk(1), vec, vec, vec],
        out_specs=[blk(0), blk(0)],
        out_shape=[jax.ShapeDtypeStruct((R, W), F32), jax.ShapeDtypeStruct((R, W), BF16)],
        compiler_params=_cparams(("arbitrary",)),
        name="gmlp_single",
    )(uv_s, uv_s, g_v.reshape(1, W), w0, b0)


def gmlp_layer(x, g_mix, w_in_all, g_v, w_s, b_s, w_o_all, j, n_batch, seq):
    Mp = n_batch * seq
    W = g_v.shape[0]
    uv = norm_mm(x, g_mix, w_in_all, (0,), w_in_all.shape[-1], 1024, _gelu_tanh, F32, "gmlp_in", layer=j)
    assert seq % C_CHUNK == 0
    bias_full = jnp.repeat(b_s[:, :C_CHUNK].T, C_GDIM, axis=1)
    v_p, z_p = gmlp_mix(uv, g_v, w_s[:, :C_CHUNK, :C_CHUNK], bias_full, Mp)
    w0 = jnp.repeat(w_s[:, 0, 0], C_GDIM).reshape(1, W)
    b0 = jnp.repeat(b_s[:, 0], C_GDIM).reshape(1, W)
    v_s, z_s = gmlp_single(uv[Mp:], g_v, w0, b0)
    z = jnp.concatenate([z_p, z_s], axis=0)
    x = mm(z, w_o_all, F32, "gmlp_out", res=x, scale=1.0, layer=j, tn=1024)
    last = ((seq - 1) // C_CHUNK) * C_CHUNK
    v_last = v_p.reshape(n_batch, seq, W)[:, last:]
    return x, v_last, v_s


def _forward(x_prompt, x_sample, cache_a_ckv, cache_a_kr, state_b, page_table, norm_ffn_a,
             ffn_a_wi, ffn_a_wo, norm_mix, a_w_in, a_g_q, a_g_kv, a_w_uq, a_w_ukv, a_w_o,
             b_w_in, b_lower_bounds, b_g_o, b_w_o, c_w_in, c_g_v, c_w_s, c_b_s, c_w_o,
             norm_ffn_b, ffn_b_wi, ffn_b_wo, final_norm, *, flash_tile, scan_tb):
    n_batch, seq, D = x_prompt.shape
    Bd, dec_seq, _ = x_sample.shape
    assert dec_seq == 1
    depth = norm_mix.shape[0]
    Mp = n_batch * seq
    past_len = page_table.shape[1] * PAGE_SIZE
    x = jnp.concatenate([x_prompt.reshape(Mp, D), x_sample.reshape(Bd, D)], axis=0)

    pos = jnp.concatenate([jnp.tile(jnp.arange(seq), n_batch),
                           jnp.full((Bd,), past_len, jnp.int32)])
    cos_t, sin_t = _rope_tables(pos)

    cache_krt = jnp.swapaxes(cache_a_kr, 2, 3)

    a_ckv, a_kr, b_sp, b_ss, c_vp, c_vs = [], [], [], [], [], []
    a_w_o, b_w_in, b_w_o, c_w_in, c_w_o = (w.astype(BF16) for w in (a_w_o, b_w_in, b_w_o, c_w_in, c_w_o))

    for i in range(depth):
        x = ffn_half_step(x, norm_ffn_a[i], ffn_a_wi, ffn_a_wo, i)
        j = i // N_MIXERS
        kind = i % N_MIXERS
        if kind == 0:
            x, ckv, kr = mla_layer(x, norm_mix[i], a_w_in[j], a_g_q[j], a_g_kv[j], a_w_uq[j],
                                   a_w_ukv[j], a_w_o, cache_a_ckv, cache_krt, page_table,
                                   j, n_batch, seq, cos_t, sin_t, flash_tile)
            a_ckv.append(ckv)
            a_kr.append(kr)
        elif kind == 1:
            x, sp, ss = hgrn_layer(x, norm_mix[i], b_w_in, b_lower_bounds, b_g_o[j], b_w_o,
                                   state_b[j], j, i, n_batch, seq, scan_tb)
            b_sp.append(sp)
            b_ss.append(ss)
        else:
            x, vp, vs = gmlp_layer(x, norm_mix[i], c_w_in, c_g_v[j], c_w_s[j], c_b_s[j],
                                   c_w_o, j, n_batch, seq)
            c_vp.append(vp)
            c_vs.append(vs)
        x = ffn_half_step(x, norm_ffn_b[i], ffn_b_wi, ffn_b_wo, i)

    y_p = final_norm_call(x, final_norm, 0, Mp, min(1024, Mp))
    y_s = final_norm_call(x, final_norm, Mp, Bd, Bd)
    ckv_all = jnp.stack(a_ckv)
    kr_all = jnp.stack(a_kr)
    return (y_p.reshape(n_batch, seq, D), y_s.reshape(Bd, 1, D),
            ckv_all[:, :Mp].reshape(-1, n_batch, seq, A_KV_LORA),
            kr_all[:, :Mp].reshape(-1, n_batch, seq, A_ROPE),
            ckv_all[:, Mp:].reshape(-1, Bd, 1, A_KV_LORA),
            kr_all[:, Mp:].reshape(-1, Bd, 1, A_ROPE),
            jnp.stack(b_sp), jnp.stack(b_ss),
            jnp.stack(c_vp), jnp.stack(c_vs)[:, :, None, :])


def kernel(x_prompt, x_sample, cache_a_ckv, cache_a_kr, state_b, page_table, norm_ffn_a, ffn_a_wi, ffn_a_wo, norm_mix, a_w_in, a_g_q, a_g_kv, a_w_uq, a_w_ukv, a_w_o, b_w_in, b_lower_bounds, b_g_o, b_w_o, c_w_in, c_g_v, c_w_s, c_b_s, c_w_o, norm_ffn_b, ffn_b_wi, ffn_b_wo, final_norm):
    seq = x_prompt.shape[1]
    return _forward(x_prompt, x_sample, cache_a_ckv, cache_a_kr, state_b, page_table, norm_ffn_a,
                    ffn_a_wi, ffn_a_wo, norm_mix, a_w_in, a_g_q, a_g_kv, a_w_uq, a_w_ukv, a_w_o,
                    b_w_in, b_lower_bounds, b_g_o, b_w_o, c_w_in, c_g_v, c_w_s, c_b_s, c_w_o,
                    norm_ffn_b, ffn_b_wi, ffn_b_wo, final_norm,
                    flash_tile=min(2048, seq // 2), scan_tb=min(256, seq))
```
